```python
import math
import jax, jax.numpy as jnp
from jax import lax
import numpy as np

D_MODEL = 4096
BATCH = 1
SEQ = 16384
DEPTH = 4

SSD_D_INNER = D_MODEL
SSD_HEAD_DIM = 64
SSD_HEADS = SSD_D_INNER // SSD_HEAD_DIM
SSD_GROUPS = 8
SSD_HPG = SSD_HEADS // SSD_GROUPS
SSD_STATE = 128
SSD_CONV = 4
CHUNK = 128
S5_WIDTH = D_MODEL // 2
S5_GROUP_CH = 16
S5_GROUPS = S5_WIDTH // S5_GROUP_CH
S5_STATE = 64
D_FF = 5632
FFN_CONV = 3
ADA_RANK = 1024
N_MOD = 6
EPS = 1e-6
DT_MIN = 0.001
DT_MAX = 0.1

XBC_WIDTH = SSD_D_INNER + 2 * SSD_GROUPS * SSD_STATE
IN_SPLITS = (SSD_D_INNER,
             SSD_D_INNER + XBC_WIDTH,
             SSD_D_INNER + XBC_WIDTH + SSD_HEADS,
             SSD_D_INNER + XBC_WIDTH + SSD_HEADS + S5_WIDTH)
IN_WIDTH = IN_SPLITS[-1] + 2 * D_MODEL

kernel_name = 'hybrid_ssd_s5_convffn_adaln'


def rmsnorm(x, w):
    x32 = x.astype(jnp.float32)
    y = x32 * lax.rsqrt(jnp.mean(x32 * x32, axis=-1, keepdims=True) + EPS)
    return y.astype(x.dtype) * w


def gated_group_rmsnorm(y, z, w):
    g = (y * jax.nn.silu(z)).astype(jnp.float32)
    g = g.reshape(*y.shape[:-1], SSD_GROUPS, -1)
    g = g * lax.rsqrt(jnp.mean(g * g, axis=-1, keepdims=True) + EPS)
    return g.reshape(y.shape).astype(y.dtype) * w


def causal_dwconv(x, w, b):
    k = w.shape[0]
    y = lax.conv_general_dilated(x, w[:, None, :].astype(x.dtype), window_strides=(1,),
                                 padding=[(k - 1, 0)],
                                 dimension_numbers=('NWC', 'WIO', 'NWC'),
                                 feature_group_count=x.shape[-1])
    return y + b


def to_chunks(a):
    b, s = a.shape[:2]
    return jnp.swapaxes(a.reshape(b, s // CHUNK, CHUNK, *a.shape[2:]), 0, 1)


def from_chunks(a):
    a = jnp.swapaxes(a, 0, 1)
    return a.reshape(a.shape[0], a.shape[1] * a.shape[2], *a.shape[3:])


def ssd_chunked(xh, dt, a, bg, cg):
    bsz, s = xh.shape[:2]
    x5 = xh.reshape(bsz, s, SSD_GROUPS, SSD_HPG, SSD_HEAD_DIM)
    dt4 = dt.reshape(bsz, s, SSD_GROUPS, SSD_HPG)
    a_gh = a.reshape(SSD_GROUPS, SSD_HPG)
    mask = jnp.tril(jnp.ones((CHUNK, CHUNK), dtype=bool))[None, :, :, None, None]

    def step(state, inp):
        xc, dtc, bc, cc = inp
        cum = jnp.cumsum(dtc * a_gh, axis=1)
        seg = cum[:, :, None] - cum[:, None, :]
        decay = jnp.exp(jnp.where(mask, seg, -jnp.inf))
        xdt = xc * dtc[..., None]
        cb = jnp.einsum('btgn,bsgn->btsg', cc, bc)
        y_intra = jnp.einsum('btsg,btsgh,bsghp->btghp', cb, decay, xdt)
        y_inter = jnp.einsum('btgn,bghpn->btghp', cc, state) * jnp.exp(cum)[..., None]
        decay_end = jnp.exp(cum[:, -1:] - cum)
        new_state = (state * jnp.exp(cum[:, -1])[..., None, None]
                     + jnp.einsum('bsgn,bsgh,bsghp->bghpn', bc, decay_end, xdt))
        return new_state, y_intra + y_inter

    state0 = jnp.zeros((bsz, SSD_GROUPS, SSD_HPG, SSD_HEAD_DIM, SSD_STATE), jnp.float32)
    _, ys = lax.scan(step, state0, (to_chunks(x5), to_chunks(dt4), to_chunks(bg), to_chunks(cg)))
    return from_chunks(ys).reshape(bsz, s, SSD_HEADS, SSD_HEAD_DIM)


def ssd_branch(z, xbc, dt_raw, conv_w, conv_b, dt_bias, a_log, d, norm_w, w_o):
    bsz, s = z.shape[:2]
    f32 = jnp.float32
    xbc = jax.nn.silu(causal_dwconv(xbc, conv_w, conv_b))
    xs, bs, cs = jnp.split(xbc, [SSD_D_INNER, SSD_D_INNER + SSD_GROUPS * SSD_STATE], axis=-1)
    xh = xs.astype(f32).reshape(bsz, s, SSD_HEADS, SSD_HEAD_DIM)
    dt = jax.nn.softplus(dt_raw.astype(f32) + dt_bias.astype(f32))
    a = -jnp.exp(a_log.astype(f32))
    bg = bs.astype(f32).reshape(bsz, s, SSD_GROUPS, SSD_STATE)
    cg = cs.astype(f32).reshape(bsz, s, SSD_GROUPS, SSD_STATE)
    y = ssd_chunked(xh, dt, a, bg, cg) + d.astype(f32)[:, None] * xh
    y = y.reshape(bsz, s, SSD_D_INNER).astype(z.dtype)
    return gated_group_rmsnorm(y, z, norm_w) @ w_o


def s5_chunked(u, lam_re, lam_im, log_dt, b_re, b_im, c_re, c_im):
    f32 = jnp.float32
    lam_re, lam_im = lam_re.astype(f32), lam_im.astype(f32)
    b_re, b_im, c_re, c_im = (t.astype(f32) for t in (b_re, b_im, c_re, c_im))
    dt = jnp.exp(log_dt.astype(f32))[:, None]
    mag = jnp.exp(lam_re * dt)
    ang = lam_im * dt
    abar_re, abar_im = mag * jnp.cos(ang), mag * jnp.sin(ang)
    den = lam_re * lam_re + lam_im * lam_im
    nr, ni = abar_re - 1.0, abar_im
    coef_re = (nr * lam_re + ni * lam_im) / den
    coef_im = (ni * lam_re - nr * lam_im) / den
    bbar_re = coef_re[..., None] * b_re - coef_im[..., None] * b_im
    bbar_im = coef_re[..., None] * b_im + coef_im[..., None] * b_re
    bsz = u.shape[0]
    a_shape = (bsz, CHUNK, S5_GROUPS, S5_STATE)
    ar = jnp.broadcast_to(abar_re, a_shape)
    ai = jnp.broadcast_to(abar_im, a_shape)

    def combine(e1, e2):
        a1r, a1i, b1r, b1i = e1
        a2r, a2i, b2r, b2i = e2
        return (a2r * a1r - a2i * a1i, a2r * a1i + a2i * a1r,
                a2r * b1r - a2i * b1i + b2r, a2r * b1i + a2i * b1r + b2i)

    def step(carry, uc):
        hr0, hi0 = carry
        bur = jnp.einsum('bqgi,gni->bqgn', uc, bbar_re)
        bui = jnp.einsum('bqgi,gni->bqgn', uc, bbar_im)
        pr, pi, sr, si = lax.associative_scan(combine, (ar, ai, bur, bui), axis=1)
        hr = sr + pr * hr0[:, None] - pi * hi0[:, None]
        hi = si + pr * hi0[:, None] + pi * hr0[:, None]
        y = (jnp.einsum('bqgn,gin->bqgi', hr, c_re)
             - jnp.einsum('bqgn,gin->bqgi', hi, c_im))
        return (hr[:, -1], hi[:, -1]), y

    h0 = jnp.zeros((bsz, S5_GROUPS, S5_STATE), f32)
    _, ys = lax.scan(step, (h0, h0), to_chunks(u))
    return from_chunks(ys)


def s5_branch(u5, lam_re, lam_im, log_dt, b_re, b_im, c_re, c_im, d, w_glu):
    bsz, s = u5.shape[:2]
    u = u5.astype(jnp.float32).reshape(bsz, s, S5_GROUPS, S5_GROUP_CH)
    y = s5_chunked(u, lam_re, lam_im, log_dt, b_re, b_im, c_re, c_im)
    y = y + d.astype(jnp.float32).reshape(S5_GROUPS, S5_GROUP_CH) * u
    g = jax.nn.gelu(y.reshape(bsz, s, S5_WIDTH)).astype(u5.dtype)
    v, gt = jnp.split(g @ w_glu, 2, axis=-1)
    return v * jax.nn.sigmoid(gt)


def setup_inputs(seed: int = 0) -> dict:
    key = jax.random.key(seed)
    ks = jax.random.split(key, 32)
    f32 = jnp.float32
    L = DEPTH

    def nrm(k, shape, scale):
        return jax.random.normal(k, shape, f32) * scale

    x = nrm(ks[0], (BATCH, SEQ, D_MODEL), 1.0)
    c = nrm(ks[1], (BATCH, D_MODEL), 1.0)
    ada_w1 = nrm(ks[2], (D_MODEL, ADA_RANK), D_MODEL ** -0.5)
    ada_w2 = nrm(ks[3], (ADA_RANK, N_MOD * D_MODEL), 0.1 * ADA_RANK ** -0.5)
    ada_b = nrm(ks[4], (N_MOD * D_MODEL,), 0.02)
    gate_rows = jnp.array([0.0, 0.0, 1.0, 0.0, 0.0, 1.0], f32)[None, :, None]
    ada_table = gate_rows + nrm(ks[5], (L, N_MOD, D_MODEL), 0.02)
    norm_mix_w = 1.0 + nrm(ks[6], (L, D_MODEL), 0.02)
    w_in = nrm(ks[7], (L, D_MODEL, IN_WIDTH), D_MODEL ** -0.5)
    ssd_conv_w = nrm(ks[8], (L, SSD_CONV, XBC_WIDTH), SSD_CONV ** -0.5)
    ssd_conv_b = nrm(ks[9], (L, XBC_WIDTH), 0.02)
    dt0 = jnp.exp(jax.random.uniform(ks[10], (L, SSD_HEADS), f32, math.log(DT_MIN), math.log(DT_MAX)))
    ssd_dt_bias = dt0 + jnp.log(-jnp.expm1(-dt0))
    ssd_a_log = jnp.log(jax.random.uniform(ks[11], (L, SSD_HEADS), f32, 1.0, 16.0))
    ssd_d = 1.0 + nrm(ks[12], (L, SSD_HEADS), 0.1)
    ssd_norm_w = 1.0 + nrm(ks[13], (L, SSD_D_INNER), 0.02)
    w_ssd_out = nrm(ks[14], (L, SSD_D_INNER, D_MODEL), SSD_D_INNER ** -0.5)
    n = jnp.arange(S5_STATE, dtype=f32)
    s5_lam_re = -0.5 + nrm(ks[15], (L, S5_GROUPS, S5_STATE), 0.01)
    s5_lam_im = math.pi * n + nrm(ks[16], (L, S5_GROUPS, S5_STATE), 0.01)
    s5_log_dt = jax.random.uniform(ks[17], (L, S5_GROUPS), f32, math.log(DT_MIN), math.log(DT_MAX))
    s5_b_re = nrm(ks[18], (L, S5_GROUPS, S5_STATE, S5_GROUP_CH), (2 * S5_GROUP_CH) ** -0.5)
    s5_b_im = nrm(ks[19], (L, S5_GROUPS, S5_STATE, S5_GROUP_CH), (2 * S5_GROUP_CH) ** -0.5)
    s5_c_re = nrm(ks[20], (L, S5_GROUPS, S5_GROUP_CH, S5_STATE), S5_STATE ** -0.5)
    s5_c_im = nrm(ks[21], (L, S5_GROUPS, S5_GROUP_CH, S5_STATE), S5_STATE ** -0.5)
    s5_d = nrm(ks[22], (L, S5_WIDTH), 1.0)
    w_glu = nrm(ks[23], (L, S5_WIDTH, 2 * D_MODEL), S5_WIDTH ** -0.5)
    w_out = nrm(ks[24], (L, D_MODEL, D_MODEL), D_MODEL ** -0.5)
    norm_ffn_w = 1.0 + nrm(ks[25], (L, D_MODEL), 0.02)
    w_up = nrm(ks[26], (L, D_MODEL, 2 * D_FF), D_MODEL ** -0.5)
    ffn_conv_w = nrm(ks[27], (L, FFN_CONV, 2 * D_FF), FFN_CONV ** -0.5)
    ffn_conv_b = nrm(ks[28], (L, 2 * D_FF), 0.02)
    w_down = nrm(ks[29], (L, D_FF, D_MODEL), D_FF ** -0.5)
    final_norm_w = 1.0 + nrm(ks[30], (D_MODEL,), 0.02)
    return {'x': x, 'c': c, 'ada_w1': ada_w1, 'ada_w2': ada_w2, 'ada_b': ada_b,
            'ada_table': ada_table, 'norm_mix_w': norm_mix_w, 'w_in': w_in,
            'ssd_conv_w': ssd_conv_w, 'ssd_conv_b': ssd_conv_b, 'ssd_dt_bias': ssd_dt_bias,
            'ssd_a_log': ssd_a_log, 'ssd_d': ssd_d, 'ssd_norm_w': ssd_norm_w,
            'w_ssd_out': w_ssd_out, 's5_lam_re': s5_lam_re, 's5_lam_im': s5_lam_im,
            's5_log_dt': s5_log_dt, 's5_b_re': s5_b_re, 's5_b_im': s5_b_im,
            's5_c_re': s5_c_re, 's5_c_im': s5_c_im, 's5_d': s5_d, 'w_glu': w_glu,
            'w_out': w_out, 'norm_ffn_w': norm_ffn_w, 'w_up': w_up,
            'ffn_conv_w': ffn_conv_w, 'ffn_conv_b': ffn_conv_b, 'w_down': w_down,
            'final_norm_w': final_norm_w}


def reference(x, c, ada_w1, ada_w2, ada_b, ada_table, norm_mix_w, w_in,
              ssd_conv_w, ssd_conv_b, ssd_dt_bias, ssd_a_log, ssd_d, ssd_norm_w,
              w_ssd_out, s5_lam_re, s5_lam_im, s5_log_dt, s5_b_re, s5_b_im,
              s5_c_re, s5_c_im, s5_d, w_glu, w_out, norm_ffn_w, w_up,
              ffn_conv_w, ffn_conv_b, w_down, final_norm_w):
    mod = ((jax.nn.silu(c) @ ada_w1) @ ada_w2 + ada_b).reshape(c.shape[0], N_MOD, D_MODEL)
    for l in range(DEPTH):
        m = (mod + ada_table[l])[:, :, None, :]
        h = rmsnorm(x, norm_mix_w[l]) * (1.0 + m[:, 1]) + m[:, 0]
        z, xbc, dt_raw, u5, gate_logits = jnp.split(h @ w_in[l], IN_SPLITS, axis=-1)
        y_ssd = ssd_branch(z, xbc, dt_raw, ssd_conv_w[l], ssd_conv_b[l], ssd_dt_bias[l],
                           ssd_a_log[l], ssd_d[l], ssd_norm_w[l], w_ssd_out[l])
        y_s5 = s5_branch(u5, s5_lam_re[l], s5_lam_im[l], s5_log_dt[l], s5_b_re[l], s5_b_im[l],
                         s5_c_re[l], s5_c_im[l], s5_d[l], w_glu[l])
        g_ssd, g_s5 = jnp.split(jax.nn.sigmoid(gate_logits), 2, axis=-1)
        x = x + m[:, 2] * ((g_ssd * y_ssd + g_s5 * y_s5) @ w_out[l])
        h = rmsnorm(x, norm_ffn_w[l]) * (1.0 + m[:, 4]) + m[:, 3]
        up = causal_dwconv(h @ w_up[l], ffn_conv_w[l], ffn_conv_b[l])
        a, v = jnp.split(up, 2, axis=-1)
        x = x + m[:, 5] * ((jax.nn.silu(a) * v) @ w_down[l])
    return rmsnorm(x, final_norm_w)
```

```python
import functools

import jax
import jax.numpy as jnp
from jax import lax
from jax.experimental import pallas as pl
from jax.experimental.pallas import tpu as pltpu

F32 = jnp.float32
BF16 = jnp.bfloat16

SSD_HEAD_DIM = 64
SSD_GROUPS = 8
SSD_STATE = 128
CHUNK = 128
S5_GROUP_CH = 16
S5_STATE = 64
N_MOD = 6
EPS = 1e-6

V7X_VMEM_BYTES = 64 * 1024 * 1024
VMEM_LIMIT_BYTES = V7X_VMEM_BYTES - 8 * 1024 * 1024
SUBLANES = 8
S5_GROUPS_PER_BLOCK = 16


def _compiler_params(semantics):
    return pltpu.CompilerParams(dimension_semantics=semantics,
                                vmem_limit_bytes=VMEM_LIMIT_BYTES)


def _mm_kernel(a_ref, b_ref, o_ref, *scratch, nk):
    if nk == 1:
        o_ref[...] = jnp.dot(a_ref[...], b_ref[...],
                             preferred_element_type=F32).astype(o_ref.dtype)
        return
    acc_ref, = scratch
    k = pl.program_id(2)

    @pl.when(k == 0)
    def _():
        acc_ref[...] = jnp.zeros_like(acc_ref)

    acc_ref[...] += jnp.dot(a_ref[...], b_ref[...], preferred_element_type=F32)

    @pl.when(k == nk - 1)
    def _():
        o_ref[...] = acc_ref[...].astype(o_ref.dtype)


def _pick_tile(dim, want):
    t = min(dim, want)
    while dim % t:
        t //= 2
    return t


def _matmul(a, b, out_dtype, tm=1024, tn=1024, tk=4096):
    m, k = a.shape
    k2, n = b.shape
    assert k == k2
    tm, tn = _pick_tile(m, tm), _pick_tile(n, tn)
    if k % tk:
        tk = k // 2 if k > tk else k
    tk = min(tk, k)
    assert k % tk == 0
    nk = k // tk
    scratch = [pltpu.VMEM((tm, tn), F32)] if nk > 1 else []
    return pl.pallas_call(
        functools.partial(_mm_kernel, nk=nk),
        grid=(m // tm, n // tn, nk),
        in_specs=[pl.BlockSpec((tm, tk), lambda i, j, kk: (i, kk)),
                  pl.BlockSpec((tk, tn), lambda i, j, kk: (kk, j))],
        out_specs=pl.BlockSpec((tm, tn), lambda i, j, kk: (i, j)),
        out_shape=jax.ShapeDtypeStruct((m, n), out_dtype),
        scratch_shapes=scratch,
        compiler_params=_compiler_params(("parallel", "parallel", "arbitrary")),
        name="matmul",
    )(a, b)


def _split3(x):
    hi = x.astype(BF16)
    r1 = x - hi.astype(F32)
    mid = r1.astype(BF16)
    lo = (r1 - mid.astype(F32)).astype(BF16)
    return hi, mid, lo


def _dot01_left(t01, x):
    hi, mid, lo = _split3(x)
    d = functools.partial(jnp.dot, preferred_element_type=F32)
    return d(t01, hi) + d(t01, mid) + d(t01, lo)


def _dot01_right(x, t01):
    hi, mid, lo = _split3(x)
    d = functools.partial(jnp.dot, preferred_element_type=F32)
    return d(hi, t01) + d(mid, t01) + d(lo, t01)


def _ssd_kernel(x_ref, b_ref, c_ref, dt_ref, dtT_ref, aexp_ref, acol_ref, e_ref, dexp_ref,
                y_ref, state_ref, *, nchunk, hpg):
    q = CHUNK
    p = SSD_HEAD_DIM

    @pl.when(pl.program_id(1) == 0)
    def _():
        state_ref[...] = jnp.zeros_like(state_ref)

    row = lax.broadcasted_iota(jnp.int32, (q, q), 0)
    col = lax.broadcasted_iota(jnp.int32, (q, q), 1)
    mask = row >= col
    tri = mask.astype(BF16)
    tri_t = (row <= col).astype(BF16)
    e = e_ref[...]
    aexp = aexp_ref[...]
    dexp = dexp_ref[...]
    acol = acol_ref[...]
    dot = functools.partial(jnp.dot, preferred_element_type=F32)

    for ci in range(nchunk):
        sl = pl.ds(ci * q, q)
        x = x_ref[sl, :].astype(F32)
        b = b_ref[sl, :]
        c = c_ref[sl, :]
        dt_exp = _dot01_right(dt_ref[sl, :], e)
        cum = _dot01_left(tri, dt_exp * aexp)
        cum_t = _dot01_right(dtT_ref[:, sl] * acol, tri_t)
        cum_last = cum[q - 1:q, :]
        xdt = x * dt_exp
        cb = lax.dot_general(c, b, (((1,), (1,)), ((), ())), preferred_element_type=F32)
        st = state_ref[...]
        y_inter = dot(c, st.astype(BF16)) * jnp.exp(cum)
        ys = []
        for h in range(hpg):
            seg = cum[:, h * p:h * p + 1] - cum_t[h:h + 1, :]
            dec = jnp.exp(jnp.where(mask, seg, -jnp.inf))
            w = (cb * dec).astype(BF16)
            ys.append(dot(w, xdt[:, h * p:(h + 1) * p].astype(BF16)))
        y_intra = jnp.concatenate(ys, axis=1)
        y_ref[sl, :] = y_intra + y_inter + dexp * x
        dec_end = jnp.exp(cum_last - cum)
        b_t = b.astype(F32).T.astype(BF16)
        state_ref[...] = st * jnp.exp(cum_last) + dot(b_t, (xdt * dec_end).astype(BF16))


def _ssd_scan(xs, bm, cm, dt, a, d, rows_per_step=512):
    s, inner = xs.shape
    h = dt.shape[1]
    g = SSD_GROUPS
    hpg = h // g
    gw = hpg * SSD_HEAD_DIM
    n = SSD_STATE
    r = _pick_tile(s, rows_per_step)
    assert r % CHUNK == 0 and inner == g * gw
    dt_t = dt.T.reshape(g, hpg, s)
    aexp = jnp.repeat(a, SSD_HEAD_DIM).reshape(g, 1, gw)
    dexp = jnp.repeat(d, SSD_HEAD_DIM).reshape(g, 1, gw)
    acol = jnp.broadcast_to(a.reshape(g, hpg, 1), (g, hpg, CHUNK))
    lane_head = jnp.arange(inner) // SSD_HEAD_DIM
    e = (jnp.arange(h)[:, None] == lane_head[None, :]).astype(BF16)
    e = e.reshape(h, g, gw).transpose(1, 0, 2)
    return pl.pallas_call(
        functools.partial(_ssd_kernel, nchunk=r // CHUNK, hpg=hpg),
        grid=(g, s // r),
        in_specs=[pl.BlockSpec((r, gw), lambda gi, t: (t, gi)),
                  pl.BlockSpec((r, n), lambda gi, t: (t, gi)),
                  pl.BlockSpec((r, n), lambda gi, t: (t, gi)),
                  pl.BlockSpec((r, h), lambda gi, t: (t, 0)),
                  pl.BlockSpec((None, hpg, r), lambda gi, t: (gi, 0, t)),
                  pl.BlockSpec((None, 1, gw), lambda gi, t: (gi, 0, 0)),
                  pl.BlockSpec((None, hpg, CHUNK), lambda gi, t: (gi, 0, 0)),
                  pl.BlockSpec((None, h, gw), lambda gi, t: (gi, 0, 0)),
                  pl.BlockSpec((None, 1, gw), lambda gi, t: (gi, 0, 0))],
        out_specs=pl.BlockSpec((r, gw), lambda gi, t: (t, gi)),
        out_shape=jax.ShapeDtypeStruct((s, inner), F32),
        scratch_shapes=[pltpu.VMEM((n, gw), F32)],
        compiler_params=_compiler_params(("parallel", "arbitrary")),
        name="ssd_scan",
    )(xs, bm, cm, dt, dt_t, aexp, acol, e, dexp)


def _s5_kernel(u_ref, bblk_ref, cblk_ref, k_ref, d_ref, o_ref, h_ref, carry_ref, *, tt):
    w = h_ref.shape[1] // 2

    @pl.when(pl.program_id(1) == 0)
    def _():
        carry_ref[...] = jnp.zeros_like(carry_ref)

    u = u_ref[...]
    h_ref[...] = jnp.dot(u, bblk_ref[...], preferred_element_type=F32)

    def body(r, carry):
        cr, ci = carry
        rows = pl.ds(pl.multiple_of(r * SUBLANES, SUBLANES), SUBLANES)
        xr = h_ref[rows, :w]
        xi = h_ref[rows, w:]
        for idx, shift in enumerate((1, 2, 4)):
            akr = k_ref[2 * idx]
            aki = k_ref[2 * idx + 1]
            sr = pltpu.roll(xr, shift, 0)
            si = pltpu.roll(xi, shift, 0)
            xr, xi = xr + akr * sr - aki * si, xi + akr * si + aki * sr
        pr = k_ref[6]
        pi = k_ref[7]
        xr, xi = xr + pr * cr - pi * ci, xi + pr * ci + pi * cr
        h_ref[rows, :w] = xr
        h_ref[rows, w:] = xi
        return xr[SUBLANES - 1:SUBLANES, :], xi[SUBLANES - 1:SUBLANES, :]

    cr, ci = lax.fori_loop(0, tt // SUBLANES, body, (carry_ref[0:1, :w], carry_ref[0:1, w:]))
    carry_ref[0:1, :w] = cr
    carry_ref[0:1, w:] = ci
    y = jnp.dot(h_ref[...].astype(BF16), cblk_ref[...], preferred_element_type=F32)
    o_ref[...] = jax.nn.gelu(y + d_ref[...] * u.astype(F32)).astype(o_ref.dtype)


def _s5_constants(lam_re, lam_im, log_dt, b_re, b_im, c_re, c_im):
    g5, n5 = lam_re.shape
    gpb = min(S5_GROUPS_PER_BLOCK, g5)
    nblk = g5 // gpb
    dt = jnp.exp(log_dt)[:, None]
    mag = jnp.exp(lam_re * dt)
    ang = lam_im * dt
    abar_re, abar_im = mag * jnp.cos(ang), mag * jnp.sin(ang)
    den = lam_re * lam_re + lam_im * lam_im
    nr, ni = abar_re - 1.0, abar_im
    coef_re = (nr * lam_re + ni * lam_im) / den
    coef_im = (ni * lam_re - nr * lam_im) / den
    bbar_re = coef_re[..., None] * b_re - coef_im[..., None] * b_im
    bbar_im = coef_re[..., None] * b_im + coef_im[..., None] * b_re

    pw = [(abar_re, abar_im)]
    for _ in range(SUBLANES - 1):
        pr, pi = pw[-1]
        pw.append((pr * abar_re - pi * abar_im, pr * abar_im + pi * abar_re))
    rows = jnp.arange(SUBLANES)[:, None]

    def lanes(v):
        return v.reshape(nblk, 1, gpb * n5)

    kinds = []
    for shift in (1, 2, 4):
        for part in pw[shift - 1]:
            kinds.append(jnp.where(rows >= shift, lanes(part), 0.0))
    for comp in (0, 1):
        kinds.append(jnp.concatenate([lanes(pw[r][comp]) for r in range(SUBLANES)], axis=1))
    kconst = jnp.stack(kinds, axis=1)

    eye = jnp.eye(gpb, dtype=F32)

    def bdiag_in(bb):
        bb = bb.reshape(nblk, gpb, n5, S5_GROUP_CH)
        m = jnp.einsum('kgni,gh->kgihn', bb, eye)
        return m.reshape(nblk, gpb * S5_GROUP_CH, gpb * n5)

    def bdiag_out(cc):
        cc = cc.reshape(nblk, gpb, S5_GROUP_CH, n5)
        m = jnp.einsum('kgin,gh->kgnhi', cc, eye)
        return m.reshape(nblk, gpb * n5, gpb * S5_GROUP_CH)

    bblk = jnp.concatenate([bdiag_in(bbar_re), bdiag_in(bbar_im)], axis=2).astype(BF16)
    cblk = jnp.concatenate([bdiag_out(c_re), -bdiag_out(c_im)], axis=1).astype(BF16)
    return bblk, cblk, kconst


def _s5_scan(u, consts, d, rows_per_step=512):
    bblk, cblk, kconst = consts
    s, w5 = u.shape
    nblk, cb, w2 = bblk.shape
    tt = _pick_tile(s, rows_per_step)
    return pl.pallas_call(
        functools.partial(_s5_kernel, tt=tt),
        grid=(nblk, s // tt),
        in_specs=[pl.BlockSpec((tt, cb), lambda gb, t: (t, gb)),
                  pl.BlockSpec((None, cb, w2), lambda gb, t: (gb, 0, 0)),
                  pl.BlockSpec((None, w2, cb), lambda gb, t: (gb, 0, 0)),
                  pl.BlockSpec((None, 8, SUBLANES, w2 // 2), lambda gb, t: (gb, 0, 0, 0)),
                  pl.BlockSpec((1, cb), lambda gb, t: (0, gb))],
        out_specs=pl.BlockSpec((tt, cb), lambda gb, t: (t, gb)),
        out_shape=jax.ShapeDtypeStruct((s, w5), BF16),
        scratch_shapes=[pltpu.VMEM((tt, w2), F32), pltpu.VMEM((SUBLANES, w2), F32)],
        compiler_params=_compiler_params(("parallel", "arbitrary")),
        name="s5_scan",
    )(u, bblk, cblk, kconst, d.reshape(1, w5))


def _rmsnorm(x, w):
    return x * lax.rsqrt(jnp.mean(x * x, axis=-1, keepdims=True) + EPS) * w


def _causal_dwconv(x, w, b):
    k = w.shape[0]
    xp = jnp.pad(x, ((k - 1, 0), (0, 0)))
    s = x.shape[0]
    out = b
    for i in range(k):
        out = out + w[i] * xp[i:i + s]
    return out


def kernel(x, c, ada_w1, ada_w2, ada_b, ada_table, norm_mix_w, w_in, ssd_conv_w, ssd_conv_b,
           ssd_dt_bias, ssd_a_log, ssd_d, ssd_norm_w, w_ssd_out, s5_lam_re, s5_lam_im, s5_log_dt,
           s5_b_re, s5_b_im, s5_c_re, s5_c_im, s5_d, w_glu, w_out, norm_ffn_w, w_up, ffn_conv_w,
           ffn_conv_b, w_down, final_norm_w):
    bsz, seq, dm = x.shape
    assert bsz == 1
    depth = w_in.shape[0]
    heads = ssd_dt_bias.shape[1]
    inner = heads * SSD_HEAD_DIM
    bc_w = SSD_GROUPS * SSD_STATE
    xbc_w = inner + 2 * bc_w
    w5 = s5_d.shape[1]
    d_ff = w_down.shape[1]
    o_dt = inner + xbc_w
    o_u5 = o_dt + heads
    o_gate = o_u5 + w5

    cpad = jnp.pad(jax.nn.silu(c), ((0, 2 * SUBLANES - bsz), (0, 0))).astype(BF16)
    t1 = _matmul(cpad, ada_w1.astype(BF16), BF16)
    mod = (_matmul(t1, ada_w2.astype(BF16), F32)[0] + ada_b).reshape(N_MOD, dm)

    xs_ = x[0]
    for l in range(depth):
        m = mod + ada_table[l]
        h = (_rmsnorm(xs_, norm_mix_w[l]) * (1.0 + m[1]) + m[0]).astype(BF16)
        wl = w_in[l]
        w_main = jnp.concatenate([wl[:, :o_dt], wl[:, o_u5:]], axis=1).astype(BF16)
        w_dt = jnp.pad(wl[:, o_dt:o_u5], ((0, 0), (0, 128 - heads))).astype(BF16)
        proj = _matmul(h, w_main, BF16)
        dt_raw = _matmul(h, w_dt, F32)[:, :heads]
        z = proj[:, :inner].astype(F32)
        xbc = proj[:, inner:o_dt].astype(F32)
        u5 = proj[:, o_dt:o_dt + w5]
        gl = proj[:, o_dt + w5:].astype(F32)

        xbc = jax.nn.silu(_causal_dwconv(xbc, ssd_conv_w[l], ssd_conv_b[l]))
        xh = xbc[:, :inner].astype(BF16)
        bm = xbc[:, inner:inner + bc_w].astype(BF16)
        cm = xbc[:, inner + bc_w:].astype(BF16)
        dt = jax.nn.softplus(dt_raw + ssd_dt_bias[l])
        a = -jnp.exp(ssd_a_log[l])
        y = _ssd_scan(xh, bm, cm, dt, a, ssd_d[l])
        gted = (y * jax.nn.silu(z)).reshape(seq, SSD_GROUPS, -1)
        gted = gted * lax.rsqrt(jnp.mean(gted * gted, axis=-1, keepdims=True) + EPS)
        gted = (gted.reshape(seq, inner) * ssd_norm_w[l]).astype(BF16)
        y_ssd = _matmul(gted, w_ssd_out[l].astype(BF16), F32)

        consts = _s5_constants(s5_lam_re[l], s5_lam_im[l], s5_log_dt[l], s5_b_re[l], s5_b_im[l],
                               s5_c_re[l], s5_c_im[l])
        g5 = _s5_scan(u5, consts, s5_d[l])
        glu = _matmul(g5, w_glu[l].astype(BF16), F32)
        y_s5 = glu[:, :dm] * jax.nn.sigmoid(glu[:, dm:])

        gates = jax.nn.sigmoid(gl)
        mix = (gates[:, :dm] * y_ssd + gates[:, dm:] * y_s5).astype(BF16)
        xs_ = xs_ + m[2] * _matmul(mix, w_out[l].astype(BF16), F32)

        h = (_rmsnorm(xs_, norm_ffn_w[l]) * (1.0 + m[4]) + m[3]).astype(BF16)
        up = _matmul(h, w_up[l].astype(BF16), F32)
        up = _causal_dwconv(up, ffn_conv_w[l], ffn_conv_b[l])
        act = (jax.nn.silu(up[:, :d_ff]) * up[:, d_ff:]).astype(BF16)
        xs_ = xs_ + m[5] * _matmul(act, w_down[l].astype(BF16), F32)
    return _rmsnorm(xs_, final_norm_w)[None]
```

```python
import functools
import math

import jax
import jax.numpy as jnp
from jax import lax
from jax.experimental import pallas as pl
from jax.experimental.pallas import tpu as pltpu

F32 = jnp.float32
BF16 = jnp.bfloat16

SSD_HEAD_DIM = 64
SSD_GROUPS = 8
SSD_STATE = 128
CHUNK = 128
S5_GROUP_CH = 16
S5_STATE = 64
N_MOD = 6
EPS = 1e-6

V7X_VMEM_BYTES = 64 * 1024 * 1024
VMEM_LIMIT_BYTES = V7X_VMEM_BYTES - 8 * 1024 * 1024
SUBLANES = 8
LANES = 128
MXU_DIM = 256
S5_GROUPS_PER_BLOCK = MXU_DIM // S5_GROUP_CH
HALO = SUBLANES


def _compiler_params(semantics):
    return pltpu.CompilerParams(dimension_semantics=semantics,
                                vmem_limit_bytes=VMEM_LIMIT_BYTES)


def _pick_tile(dim, want):
    t = min(dim, want)
    while dim % t:
        t //= 2
    return t


def _dot(a, b):
    return jnp.dot(a, b, preferred_element_type=F32)


def _sigmoid(x):
    return jax.nn.sigmoid(x)


def _mm_kernel(a_ref, b_ref, o_ref, *scratch, nk):
    if nk == 1:
        o_ref[...] = _dot(a_ref[...], b_ref[...]).astype(o_ref.dtype)
        return
    acc_ref, = scratch
    k = pl.program_id(2)

    @pl.when(k == 0)
    def _():
        acc_ref[...] = jnp.zeros_like(acc_ref)

    acc_ref[...] += _dot(a_ref[...], b_ref[...])

    @pl.when(k == nk - 1)
    def _():
        o_ref[...] = acc_ref[...].astype(o_ref.dtype)


def _k_tiling(k, tk):
    if k % tk:
        tk = k // 2 if k > tk else k
    tk = min(tk, k)
    assert k % tk == 0
    return tk, k // tk


def _matmul(a, b, out_dtype, tm=1024, tn=1024, tk=4096):
    m, k = a.shape
    n = b.shape[1]
    tm, tn = _pick_tile(m, tm), _pick_tile(n, tn)
    tk, nk = _k_tiling(k, tk)
    scratch = [pltpu.VMEM((tm, tn), F32)] if nk > 1 else []
    return pl.pallas_call(
        functools.partial(_mm_kernel, nk=nk),
        grid=(m // tm, n // tn, nk),
        in_specs=[pl.BlockSpec((tm, tk), lambda i, j, kk: (i, kk)),
                  pl.BlockSpec((tk, tn), lambda i, j, kk: (kk, j))],
        out_specs=pl.BlockSpec((tm, tn), lambda i, j, kk: (i, j)),
        out_shape=jax.ShapeDtypeStruct((m, n), out_dtype),
        scratch_shapes=scratch,
        compiler_params=_compiler_params(("parallel", "parallel", "arbitrary")),
        name="matmul",
    )(a, b)


def _col_slices(tn):
    cs = min(tn, MXU_DIM)
    return [slice(c, c + cs) for c in range(0, tn, cs)]


def _mm_glu_kernel(a_ref, bv_ref, bg_ref, o_ref):
    a = a_ref[...]
    for sl in _col_slices(o_ref.shape[1]):
        v = _dot(a, bv_ref[:, sl])
        g = _dot(a, bg_ref[:, sl])
        o_ref[:, sl] = (v * _sigmoid(g)).astype(o_ref.dtype)


def _matmul_glu(a, w, out_dtype, tm=1024, tn=1024):
    m, k = a.shape
    n = w.shape[1] // 2
    tm, tn = _pick_tile(m, tm), _pick_tile(n, tn)
    nj = n // tn
    return pl.pallas_call(
        _mm_glu_kernel,
        grid=(m // tm, nj),
        in_specs=[pl.BlockSpec((tm, k), lambda i, j: (i, 0)),
                  pl.BlockSpec((k, tn), lambda i, j: (0, j)),
                  pl.BlockSpec((k, tn), lambda i, j: (0, j + nj))],
        out_specs=pl.BlockSpec((tm, tn), lambda i, j: (i, j)),
        out_shape=jax.ShapeDtypeStruct((m, n), out_dtype),
        compiler_params=_compiler_params(("parallel", "parallel")),
        name="matmul_glu",
    )(a, w, w)


def _mm_mix_kernel(a_ref, b_ref, g1_ref, g2_ref, y2_ref, o_ref):
    a = a_ref[...]
    for sl in _col_slices(o_ref.shape[1]):
        y1 = _dot(a, b_ref[:, sl])
        o_ref[:, sl] = (_sigmoid(g1_ref[:, sl].astype(F32)) * y1
                        + _sigmoid(g2_ref[:, sl].astype(F32)) * y2_ref[:, sl].astype(F32)
                        ).astype(o_ref.dtype)


def _matmul_mix(a, w, proj, gate_col, y2, out_dtype, tm=1024):
    m, k = a.shape
    n = w.shape[1]
    tm = _pick_tile(m, tm)
    tn = math.gcd(math.gcd(512, gate_col), n)
    nj = n // tn
    g0 = gate_col // tn
    return pl.pallas_call(
        _mm_mix_kernel,
        grid=(m // tm, nj),
        in_specs=[pl.BlockSpec((tm, k), lambda i, j: (i, 0)),
                  pl.BlockSpec((k, tn), lambda i, j: (0, j)),
                  pl.BlockSpec((tm, tn), lambda i, j: (i, g0 + j)),
                  pl.BlockSpec((tm, tn), lambda i, j: (i, g0 + nj + j)),
                  pl.BlockSpec((tm, tn), lambda i, j: (i, j))],
        out_specs=pl.BlockSpec((tm, tn), lambda i, j: (i, j)),
        out_shape=jax.ShapeDtypeStruct((m, n), out_dtype),
        compiler_params=_compiler_params(("parallel", "parallel")),
        name="matmul_mix",
    )(a, w, proj, proj, y2)


def _mm_resid_kernel(a_ref, b_ref, x_ref, g_ref, o_ref, *scratch, nk):
    if nk == 1:
        a = a_ref[...]
        for sl in _col_slices(o_ref.shape[1]):
            o_ref[:, sl] = x_ref[:, sl] + g_ref[:, sl] * _dot(a, b_ref[:, sl])
        return
    acc_ref, = scratch
    k = pl.program_id(2)

    @pl.when(k == 0)
    def _():
        acc_ref[...] = jnp.zeros_like(acc_ref)

    acc_ref[...] += _dot(a_ref[...], b_ref[...])

    @pl.when(k == nk - 1)
    def _():
        o_ref[...] = x_ref[...] + g_ref[...] * acc_ref[...]


def _matmul_resid(a, w, x, gate, tm=1024, tn=512, tk=4096):
    m, k = a.shape
    n = w.shape[1]
    tm, tn = _pick_tile(m, tm), _pick_tile(n, tn)
    tk, nk = _k_tiling(k, tk)
    scratch = [pltpu.VMEM((tm, tn), F32)] if nk > 1 else []
    return pl.pallas_call(
        functools.partial(_mm_resid_kernel, nk=nk),
        grid=(m // tm, n // tn, nk),
        in_specs=[pl.BlockSpec((tm, tk), lambda i, j, kk: (i, kk)),
                  pl.BlockSpec((tk, tn), lambda i, j, kk: (kk, j)),
                  pl.BlockSpec((tm, tn), lambda i, j, kk: (i, j)),
                  pl.BlockSpec((1, tn), lambda i, j, kk: (0, j))],
        out_specs=pl.BlockSpec((tm, tn), lambda i, j, kk: (i, j)),
        out_shape=jax.ShapeDtypeStruct((m, n), F32),
        scratch_shapes=scratch,
        compiler_params=_compiler_params(("parallel", "parallel", "arbitrary")),
        name="matmul_resid",
    )(a, w, x, gate.reshape(1, n))


def _conv_taps(buf_ref, w_ref, b_ref, base, rows, sl):
    k = w_ref.shape[0]
    out = b_ref[:, sl] + w_ref[k - 1:k, sl] * buf_ref[pl.ds(base, rows), sl]
    for j in range(1, k):
        out = out + w_ref[k - 1 - j:k - j, sl] * buf_ref[pl.ds(base - j, rows), sl]
    return out


def _mm_convgate_kernel(a_ref, ba_ref, bv_ref, wa_ref, wv_ref, ca_ref, cv_ref, o_ref,
                        ua_ref, uv_ref, halo_ref):
    i = pl.program_id(0)
    j = pl.program_id(1)
    tm = a_ref.shape[0]

    @pl.when(i == 0)
    def _():
        halo_ref[j] = jnp.zeros(halo_ref.shape[1:], F32)

    ua_ref[0:HALO, :] = halo_ref[j, 0]
    uv_ref[0:HALO, :] = halo_ref[j, 1]
    a = a_ref[...]
    for sl in _col_slices(o_ref.shape[1]):
        ua_ref[HALO:, sl] = _dot(a, ba_ref[:, sl])
        uv_ref[HALO:, sl] = _dot(a, bv_ref[:, sl])
        ca = _conv_taps(ua_ref, wa_ref, ca_ref, HALO, tm, sl)
        cv = _conv_taps(uv_ref, wv_ref, cv_ref, HALO, tm, sl)
        o_ref[:, sl] = (ca * _sigmoid(ca) * cv).astype(o_ref.dtype)
    halo_ref[j, 0] = ua_ref[tm:tm + HALO, :]
    halo_ref[j, 1] = uv_ref[tm:tm + HALO, :]


def _matmul_convgate(a, w, conv_w, conv_b, out_dtype, tm=1024):
    m, k = a.shape
    n = w.shape[1] // 2
    kc = conv_w.shape[0]
    assert kc - 1 <= HALO
    tm = _pick_tile(m, tm)
    tn = math.gcd(512, n)
    nj = n // tn
    conv_b = conv_b.reshape(1, 2 * n)
    return pl.pallas_call(
        _mm_convgate_kernel,
        grid=(m // tm, nj),
        in_specs=[pl.BlockSpec((tm, k), lambda i, j: (i, 0)),
                  pl.BlockSpec((k, tn), lambda i, j: (0, j)),
                  pl.BlockSpec((k, tn), lambda i, j: (0, j + nj)),
                  pl.BlockSpec((kc, tn), lambda i, j: (0, j)),
                  pl.BlockSpec((kc, tn), lambda i, j: (0, j + nj)),
                  pl.BlockSpec((1, tn), lambda i, j: (0, j)),
                  pl.BlockSpec((1, tn), lambda i, j: (0, j + nj))],
        out_specs=pl.BlockSpec((tm, tn), lambda i, j: (i, j)),
        out_shape=jax.ShapeDtypeStruct((m, n), out_dtype),
        scratch_shapes=[pltpu.VMEM((tm + HALO, tn), F32), pltpu.VMEM((tm + HALO, tn), F32),
                        pltpu.VMEM((nj, 2, HALO, tn), F32)],
        compiler_params=_compiler_params(("arbitrary", "arbitrary")),
        name="matmul_convgate",
    )(a, w, w, conv_w, conv_w, conv_b, conv_b)


def _norm_kernel(x_ref, w_ref, s_ref, o_ref):
    x = x_ref[...]
    ms = jnp.mean(x * x, axis=-1, keepdims=True)
    o_ref[...] = (x * lax.rsqrt(ms + EPS) * w_ref[...] + s_ref[...]).astype(o_ref.dtype)


def _norm_mod(x, w, shift, out_dtype, tr=256):
    s, d = x.shape
    tr = _pick_tile(s, tr)
    return pl.pallas_call(
        _norm_kernel,
        grid=(s // tr,),
        in_specs=[pl.BlockSpec((tr, d), lambda i: (i, 0)),
                  pl.BlockSpec((1, d), lambda i: (0, 0)),
                  pl.BlockSpec((1, d), lambda i: (0, 0))],
        out_specs=pl.BlockSpec((tr, d), lambda i: (i, 0)),
        out_shape=jax.ShapeDtypeStruct((s, d), out_dtype),
        compiler_params=_compiler_params(("parallel",)),
        name="norm_mod",
    )(x, w.reshape(1, d), shift.reshape(1, d))


def _split3(x):
    hi = x.astype(BF16)
    r1 = x - hi.astype(F32)
    mid = r1.astype(BF16)
    lo = (r1 - mid.astype(F32)).astype(BF16)
    return hi, mid, lo


def _dot01_left(t01, x):
    hi, mid, lo = _split3(x)
    return _dot(t01, hi) + _dot(t01, mid) + _dot(t01, lo)


def _dot01_right(x, t01):
    hi, mid, lo = _split3(x)
    return _dot(hi, t01) + _dot(mid, t01) + _dot(lo, t01)


def _softplus(x):
    return jnp.maximum(x, 0.0) + jnp.log1p(jnp.exp(-jnp.abs(x)))


def _ssd_kernel(x_ref, b_ref, c_ref, z_ref, dt_ref, dtT_ref, bias_ref, biasT_ref,
                wx_ref, wb_ref, wc_ref, cbx_ref, cbb_ref, cbc_ref,
                aexp_ref, acol_ref, e_ref, dexp_ref, nw_ref,
                o_ref, state_ref, xbuf_ref, bbuf_ref, cbuf_ref, *, nchunk, hpg):
    q = CHUNK
    p = SSD_HEAD_DIM
    r = x_ref.shape[0]

    @pl.when(pl.program_id(1) == 0)
    def _():
        state_ref[...] = jnp.zeros_like(state_ref)
        xbuf_ref[0:HALO, :] = jnp.zeros((HALO, xbuf_ref.shape[1]), F32)
        bbuf_ref[0:HALO, :] = jnp.zeros((HALO, bbuf_ref.shape[1]), F32)
        cbuf_ref[0:HALO, :] = jnp.zeros((HALO, cbuf_ref.shape[1]), F32)

    xbuf_ref[HALO:, :] = x_ref[...].astype(F32)
    bbuf_ref[HALO:, :] = b_ref[...].astype(F32)
    cbuf_ref[HALO:, :] = c_ref[...].astype(F32)

    row = lax.broadcasted_iota(jnp.int32, (q, q), 0)
    col = lax.broadcasted_iota(jnp.int32, (q, q), 1)
    mask = row >= col
    tri = mask.astype(BF16)
    tri_t = (row <= col).astype(BF16)
    e = e_ref[...]
    aexp = aexp_ref[...]
    dexp = dexp_ref[...]
    acol = acol_ref[...]
    nw = nw_ref[...]
    full = slice(None)

    def conv_silu(buf_ref, w_ref, cb_ref, base):
        v = _conv_taps(buf_ref, w_ref, cb_ref, base, q, full)
        return v * _sigmoid(v)

    for ci in range(nchunk):
        sl = pl.ds(ci * q, q)
        base = HALO + ci * q
        x = conv_silu(xbuf_ref, wx_ref, cbx_ref, base)
        b = conv_silu(bbuf_ref, wb_ref, cbb_ref, base).astype(BF16)
        c = conv_silu(cbuf_ref, wc_ref, cbc_ref, base).astype(BF16)
        dt = _softplus(dt_ref[sl, :] + bias_ref[...])
        dt_t = _softplus(dtT_ref[:, sl] + biasT_ref[...])
        dt_exp = _dot01_right(dt, e)
        cum = _dot01_left(tri, dt_exp * aexp)
        cum_t = _dot01_right(dt_t * acol, tri_t)
        cum_last = cum[q - 1:q, :]
        xdt = x * dt_exp
        cb = lax.dot_general(c, b, (((1,), (1,)), ((), ())), preferred_element_type=F32)
        st = state_ref[...]
        y_inter = _dot(c, st.astype(BF16)) * jnp.exp(cum)
        ys = []
        for h in range(hpg):
            seg = cum[:, h * p:h * p + 1] - cum_t[h:h + 1, :]
            dec = jnp.exp(jnp.where(mask, seg, -jnp.inf))
            w = (cb * dec).astype(BF16)
            ys.append(_dot(w, xdt[:, h * p:(h + 1) * p].astype(BF16)))
        y = jnp.concatenate(ys, axis=1) + y_inter + dexp * x
        dec_end = jnp.exp(cum_last - cum)
        b_t = b.astype(F32).T.astype(BF16)
        state_ref[...] = st * jnp.exp(cum_last) + _dot(b_t, (xdt * dec_end).astype(BF16))
        zf = z_ref[sl, :].astype(F32)
        gt = y * (zf * _sigmoid(zf))
        ms = jnp.mean(gt * gt, axis=-1, keepdims=True)
        o_ref[sl, :] = (gt * lax.rsqrt(ms + EPS) * nw).astype(o_ref.dtype)

    xbuf_ref[0:HALO, :] = xbuf_ref[r:r + HALO, :]
    bbuf_ref[0:HALO, :] = bbuf_ref[r:r + HALO, :]
    cbuf_ref[0:HALO, :] = cbuf_ref[r:r + HALO, :]


def _ssd_branch(proj, dt_raw, conv_w, conv_b, dt_bias, a_log, d, norm_w, inner, rows_per_step=512):
    s = proj.shape[0]
    h = dt_bias.shape[0]
    g = SSD_GROUPS
    hpg = h // g
    gw = hpg * SSD_HEAD_DIM
    n = SSD_STATE
    r = _pick_tile(s, rows_per_step)
    kc = conv_w.shape[0]
    assert r % CHUNK == 0 and inner == g * gw and gw % LANES == 0 and h <= LANES and kc - 1 <= HALO
    a = -jnp.exp(a_log)
    dt_t = dt_raw[:, :h].T.reshape(g, hpg, s)
    bias = jnp.pad(dt_bias, (0, LANES - h)).reshape(1, LANES)
    bias_t = jnp.broadcast_to(dt_bias.reshape(g, hpg, 1), (g, hpg, CHUNK))
    aexp = jnp.repeat(a, SSD_HEAD_DIM).reshape(g, 1, gw)
    dexp = jnp.repeat(d, SSD_HEAD_DIM).reshape(g, 1, gw)
    acol = jnp.broadcast_to(a.reshape(g, hpg, 1), (g, hpg, CHUNK))
    lane_head = jnp.arange(inner) // SSD_HEAD_DIM
    e = (jnp.arange(LANES)[:, None] == lane_head[None, :]).astype(BF16)
    e = e.reshape(LANES, g, gw).transpose(1, 0, 2)
    conv_b = conv_b.reshape(1, -1)
    xb0 = inner // gw
    bb0 = 2 * inner // n
    cb0 = bb0 + g
    wb0 = inner // n
    wc0 = wb0 + g
    per_g = lambda gi, t: (gi, 0, 0)
    return pl.pallas_call(
        functools.partial(_ssd_kernel, nchunk=r // CHUNK, hpg=hpg),
        grid=(g, s // r),
        in_specs=[pl.BlockSpec((r, gw), lambda gi, t: (t, xb0 + gi)),
                  pl.BlockSpec((r, n), lambda gi, t: (t, bb0 + gi)),
                  pl.BlockSpec((r, n), lambda gi, t: (t, cb0 + gi)),
                  pl.BlockSpec((r, gw), lambda gi, t: (t, gi)),
                  pl.BlockSpec((r, LANES), lambda gi, t: (t, 0)),
                  pl.BlockSpec((None, hpg, r), lambda gi, t: (gi, 0, t)),
                  pl.BlockSpec((1, LANES), lambda gi, t: (0, 0)),
                  pl.BlockSpec((None, hpg, CHUNK), per_g),
                  pl.BlockSpec((kc, gw), lambda gi, t: (0, gi)),
                  pl.BlockSpec((kc, n), lambda gi, t: (0, wb0 + gi)),
                  pl.BlockSpec((kc, n), lambda gi, t: (0, wc0 + gi)),
                  pl.BlockSpec((1, gw), lambda gi, t: (0, gi)),
                  pl.BlockSpec((1, n), lambda gi, t: (0, wb0 + gi)),
                  pl.BlockSpec((1, n), lambda gi, t: (0, wc0 + gi)),
                  pl.BlockSpec((None, 1, gw), per_g),
                  pl.BlockSpec((None, hpg, CHUNK), per_g),
                  pl.BlockSpec((None, LANES, gw), per_g),
                  pl.BlockSpec((None, 1, gw), per_g),
                  pl.BlockSpec((1, gw), lambda gi, t: (0, gi))],
        out_specs=pl.BlockSpec((r, gw), lambda gi, t: (t, gi)),
        out_shape=jax.ShapeDtypeStruct((s, inner), BF16),
        scratch_shapes=[pltpu.VMEM((n, gw), F32), pltpu.VMEM((HALO + r, gw), F32),
                        pltpu.VMEM((HALO + r, n), F32), pltpu.VMEM((HALO + r, n), F32)],
        compiler_params=_compiler_params(("parallel", "arbitrary")),
        name="ssd_branch",
    )(proj, proj, proj, proj, dt_raw, dt_t, bias, bias_t, conv_w, conv_w, conv_w,
      conv_b, conv_b, conv_b, aexp, acol, e, dexp, norm_w.reshape(1, inner))


def _cmul(ar, ai, br, bi):
    return ar * br - ai * bi, ar * bi + ai * br


def _s5_kernel(u_ref, pm_ref, pmt_ref, bblk_ref, cblk_ref, k_ref, p_ref, d_ref, o_ref,
               h_ref, hb_ref, hin_ref, carry_ref):
    tt = u_ref.shape[0]
    w = h_ref.shape[1] // 2
    ns = SUBLANES
    ln = tt // ns

    @pl.when(pl.program_id(1) == 0)
    def _():
        carry_ref[...] = jnp.zeros_like(carry_ref)

    u = _dot(pm_ref[...], u_ref[...]).astype(BF16)
    h_ref[...] = _dot(u, bblk_ref[...])

    def scan(q, carry):
        hr, hi = carry
        rows = pl.ds(pl.multiple_of(q * ns, ns), ns)
        pr, pi = _cmul(k_ref[0], k_ref[1], hr, hi)
        hr = pr + h_ref[rows, :w]
        hi = pi + h_ref[rows, w:]
        h_ref[rows, :w] = hr
        h_ref[rows, w:] = hi
        return hr, hi

    zero = jnp.zeros((ns, w), F32)
    er, ei = lax.fori_loop(0, ln, scan, (zero, zero))

    alr = k_ref[2, 0:1, :]
    ali = k_ref[3, 0:1, :]
    sr = carry_ref[0:1, :w]
    si = carry_ref[0:1, w:]
    for s in range(ns):
        for half in range(2):
            hin_ref[half * ns + s:half * ns + s + 1, :w] = sr
            hin_ref[half * ns + s:half * ns + s + 1, w:] = si
        nr, ni = _cmul(alr, ali, sr, si)
        sr = nr + er[s:s + 1]
        si = ni + ei[s:s + 1]
    carry_ref[0:1, :w] = sr
    carry_ref[0:1, w:] = si

    def fixup(q2, _):
        rows = pl.ds(pl.multiple_of(q2 * 2 * ns, 2 * ns), 2 * ns)
        cr, ci = _cmul(p_ref[0, rows, :], p_ref[1, rows, :], hin_ref[:, :w], hin_ref[:, w:])
        hb_ref[rows, :w] = (h_ref[rows, :w] + cr).astype(BF16)
        hb_ref[rows, w:] = (h_ref[rows, w:] + ci).astype(BF16)
        return 0

    lax.fori_loop(0, ln // 2, fixup, 0)

    y = _dot(hb_ref[...], cblk_ref[...])
    g = jax.nn.gelu(y + d_ref[...] * u.astype(F32)).astype(BF16)
    o_ref[...] = _dot(pmt_ref[...], g).astype(o_ref.dtype)


def _s5_constants(lam_re, lam_im, log_dt, b_re, b_im, c_re, c_im, ln):
    g5, n5 = lam_re.shape
    gpb = min(S5_GROUPS_PER_BLOCK, g5)
    nblk = g5 // gpb
    dt = jnp.exp(log_dt)[:, None]
    mag = jnp.exp(lam_re * dt)
    ang = lam_im * dt
    abar_re, abar_im = mag * jnp.cos(ang), mag * jnp.sin(ang)
    den = lam_re * lam_re + lam_im * lam_im
    nr, ni = abar_re - 1.0, abar_im
    coef_re = (nr * lam_re + ni * lam_im) / den
    coef_im = (ni * lam_re - nr * lam_im) / den
    bbar_re = coef_re[..., None] * b_re - coef_im[..., None] * b_im
    bbar_im = coef_re[..., None] * b_im + coef_im[..., None] * b_re

    assert ln & (ln - 1) == 0
    pr, pi = abar_re[None], abar_im[None]
    sr, si = abar_re, abar_im
    while pr.shape[0] < ln:
        qr, qi = _cmul(pr, pi, sr, si)
        pr, pi = jnp.concatenate([pr, qr]), jnp.concatenate([pi, qi])
        sr, si = _cmul(sr, si, sr, si)

    def lanes(v):
        lead = v.shape[:-2]
        v = v.reshape(lead + (nblk, gpb * n5))
        return jnp.moveaxis(v, -2, 0)

    def rep8(v):
        return jnp.broadcast_to(lanes(v)[:, None, :], (nblk, SUBLANES, gpb * n5))

    kconst = jnp.stack([rep8(abar_re), rep8(abar_im), rep8(sr), rep8(si)], axis=1)
    ptab = jnp.stack([lanes(pr), lanes(pi)], axis=1)
    ptab = jnp.repeat(ptab, SUBLANES, axis=2)

    eye = jnp.eye(gpb, dtype=F32)

    def bdiag_in(bb):
        bb = bb.reshape(nblk, gpb, n5, S5_GROUP_CH)
        m = jnp.einsum('kgni,gh->kgihn', bb, eye)
        return m.reshape(nblk, gpb * S5_GROUP_CH, gpb * n5)

    def bdiag_out(cc):
        cc = cc.reshape(nblk, gpb, S5_GROUP_CH, n5)
        m = jnp.einsum('kgin,gh->kgnhi', cc, eye)
        return m.reshape(nblk, gpb * n5, gpb * S5_GROUP_CH)

    bblk = jnp.concatenate([bdiag_in(bbar_re), bdiag_in(bbar_im)], axis=2).astype(BF16)
    cblk = jnp.concatenate([bdiag_out(c_re), -bdiag_out(c_im)], axis=1).astype(BF16)
    return bblk, cblk, kconst, ptab


def _s5_branch(proj, u_col, params, d, rows_per_step=1024):
    s = proj.shape[0]
    w5 = d.shape[0]
    tt = _pick_tile(s, rows_per_step)
    bblk, cblk, kconst, ptab = _s5_constants(*params, ln=tt // SUBLANES)
    nblk, cb, w2 = bblk.shape
    ub0 = u_col // cb
    assert u_col % cb == 0
    p_idx = jnp.arange(tt)
    src = (p_idx % SUBLANES) * (tt // SUBLANES) + p_idx // SUBLANES
    pm = (src[:, None] == p_idx[None, :]).astype(BF16)
    per_b = lambda gb, t: (gb, 0, 0)
    whole = lambda gb, t: (0, 0)
    return pl.pallas_call(
        _s5_kernel,
        grid=(nblk, s // tt),
        in_specs=[pl.BlockSpec((tt, cb), lambda gb, t: (t, ub0 + gb)),
                  pl.BlockSpec((tt, tt), whole),
                  pl.BlockSpec((tt, tt), whole),
                  pl.BlockSpec((None, cb, w2), per_b),
                  pl.BlockSpec((None, w2, cb), per_b),
                  pl.BlockSpec((None, 4, SUBLANES, w2 // 2), lambda gb, t: (gb, 0, 0, 0)),
                  pl.BlockSpec((None, 2, tt, w2 // 2), lambda gb, t: (gb, 0, 0, 0)),
                  pl.BlockSpec((1, cb), lambda gb, t: (0, gb))],
        out_specs=pl.BlockSpec((tt, cb), lambda gb, t: (t, gb)),
        out_shape=jax.ShapeDtypeStruct((s, w5), BF16),
        scratch_shapes=[pltpu.VMEM((tt, w2), F32), pltpu.VMEM((tt, w2), BF16),
                        pltpu.VMEM((2 * SUBLANES, w2), F32), pltpu.VMEM((SUBLANES, w2), F32)],
        compiler_params=_compiler_params(("parallel", "arbitrary")),
        name="s5_branch",
    )(proj, pm, pm.T, bblk, cblk, kconst, ptab, d.reshape(1, w5))


def kernel(x, c, ada_w1, ada_w2, ada_b, ada_table, norm_mix_w, w_in, ssd_conv_w, ssd_conv_b,
           ssd_dt_bias, ssd_a_log, ssd_d, ssd_norm_w, w_ssd_out, s5_lam_re, s5_lam_im, s5_log_dt,
           s5_b_re, s5_b_im, s5_c_re, s5_c_im, s5_d, w_glu, w_out, norm_ffn_w, w_up, ffn_conv_w,
           ffn_conv_b, w_down, final_norm_w):
    bsz, seq, dm = x.shape
    assert bsz == 1
    depth = w_in.shape[0]
    heads = ssd_dt_bias.shape[1]
    inner = heads * SSD_HEAD_DIM
    xbc_w = inner + 2 * SSD_GROUPS * SSD_STATE
    w5 = s5_d.shape[1]
    o_dt = inner + xbc_w
    o_u5 = o_dt + heads

    cpad = jnp.pad(jax.nn.silu(c), ((0, 2 * SUBLANES - bsz), (0, 0))).astype(BF16)
    t1 = _matmul(cpad, ada_w1.astype(BF16), BF16)
    mod = (_matmul(t1, ada_w2.astype(BF16), F32)[0] + ada_b).reshape(N_MOD, dm)

    xs = x[0]
    for l in range(depth):
        m = mod + ada_table[l]
        h = _norm_mod(xs, norm_mix_w[l] * (1.0 + m[1]), m[0], BF16)
        wl = w_in[l]
        w_main = jnp.concatenate([wl[:, :o_dt], wl[:, o_u5:]], axis=1).astype(BF16)
        w_dt = jnp.pad(wl[:, o_dt:o_u5], ((0, 0), (0, LANES - heads))).astype(BF16)
        proj = _matmul(h, w_main, BF16)
        dt_raw = _matmul(h, w_dt, F32)
        g_ssd = _ssd_branch(proj, dt_raw, ssd_conv_w[l], ssd_conv_b[l], ssd_dt_bias[l],
                            ssd_a_log[l], ssd_d[l], ssd_norm_w[l], inner)
        s5_params = (s5_lam_re[l], s5_lam_im[l], s5_log_dt[l], s5_b_re[l], s5_b_im[l],
                     s5_c_re[l], s5_c_im[l])
        g_s5 = _s5_branch(proj, o_dt, s5_params, s5_d[l])
        y_s5 = _matmul_glu(g_s5, w_glu[l].astype(BF16), BF16)
        mix = _matmul_mix(g_ssd, w_ssd_out[l].astype(BF16), proj, o_dt + w5, y_s5, BF16)
        xs = _matmul_resid(mix, w_out[l].astype(BF16), xs, m[2])
        h = _norm_mod(xs, norm_ffn_w[l] * (1.0 + m[4]), m[3], BF16)
        act = _matmul_convgate(h, w_up[l].astype(BF16), ffn_conv_w[l], ffn_conv_b[l], BF16)
        xs = _matmul_resid(act, w_down[l].astype(BF16), xs, m[5])
    out = _norm_mod(xs, final_norm_w, jnp.zeros_like(final_norm_w), F32)
    return out[None]
```

```python
import functools
import math

import jax
import jax.numpy as jnp
from jax import lax
from jax.experimental import pallas as pl
from jax.experimental.pallas import tpu as pltpu

F32 = jnp.float32
BF16 = jnp.bfloat16

SSD_HEAD_DIM = 64
SSD_GROUPS = 8
SSD_STATE = 128
CHUNK = 128
S5_GROUP_CH = 16
N_MOD = 6
EPS = 1e-6

V7X_VMEM_BYTES = 64 * 1024 * 1024
VMEM_LIMIT_BYTES = V7X_VMEM_BYTES - 8 * 1024 * 1024
SUBLANES = 8
LANES = 128
MXU_DIM = 256
S5_GROUPS_PER_BLOCK = MXU_DIM // S5_GROUP_CH
HALO = SUBLANES


def _compiler_params(semantics):
    return pltpu.CompilerParams(dimension_semantics=semantics,
                                vmem_limit_bytes=VMEM_LIMIT_BYTES)


def _pick_tile(dim, want):
    t = min(dim, want)
    while dim % t:
        t //= 2
    return t


def _dot(a, b):
    return jnp.dot(a, b, preferred_element_type=F32)


def _sigmoid(x):
    return jax.nn.sigmoid(x)


def _mm_kernel(a_ref, b_ref, o_ref, *scratch, nk):
    if nk == 1:
        o_ref[...] = _dot(a_ref[...], b_ref[...]).astype(o_ref.dtype)
        return
    acc_ref, = scratch
    k = pl.program_id(2)

    @pl.when(k == 0)
    def _():
        acc_ref[...] = jnp.zeros_like(acc_ref)

    acc_ref[...] += _dot(a_ref[...], b_ref[...])

    @pl.when(k == nk - 1)
    def _():
        o_ref[...] = acc_ref[...].astype(o_ref.dtype)


def _k_tiling(k, tk):
    if k % tk:
        tk = k // 2 if k > tk else k
    tk = min(tk, k)
    assert k % tk == 0
    return tk, k // tk


def _matmul(a, b, out_dtype, tm=1024, tn=1024, tk=4096):
    m, k = a.shape
    n = b.shape[1]
    tm, tn = _pick_tile(m, tm), _pick_tile(n, tn)
    tk, nk = _k_tiling(k, tk)
    scratch = [pltpu.VMEM((tm, tn), F32)] if nk > 1 else []
    return pl.pallas_call(
        functools.partial(_mm_kernel, nk=nk),
        grid=(m // tm, n // tn, nk),
        in_specs=[pl.BlockSpec((tm, tk), lambda i, j, kk: (i, kk)),
                  pl.BlockSpec((tk, tn), lambda i, j, kk: (kk, j))],
        out_specs=pl.BlockSpec((tm, tn), lambda i, j, kk: (i, j)),
        out_shape=jax.ShapeDtypeStruct((m, n), out_dtype),
        scratch_shapes=scratch,
        compiler_params=_compiler_params(("parallel", "parallel", "arbitrary")),
        name="matmul",
    )(a, b)


def _col_slices(tn):
    cs = min(tn, MXU_DIM)
    return [slice(c, c + cs) for c in range(0, tn, cs)]


def _mm_slab_kernel(a_ref, b_ref, o_ref):
    acc = _dot(a_ref[...], b_ref[...])
    cb = o_ref.shape[2]
    for sidx in range(o_ref.shape[0]):
        o_ref[sidx] = acc[:, sidx * cb:(sidx + 1) * cb].astype(o_ref.dtype)


def _matmul_slabs(a, w, cb, out_dtype, tm=1024, tn=1024):
    m, k = a.shape
    n = w.shape[1]
    tm, tn = _pick_tile(m, tm), _pick_tile(n, tn)
    assert tn % cb == 0
    spb = tn // cb
    return pl.pallas_call(
        _mm_slab_kernel,
        grid=(m // tm, n // tn),
        in_specs=[pl.BlockSpec((tm, k), lambda i, j: (i, 0)),
                  pl.BlockSpec((k, tn), lambda i, j: (0, j))],
        out_specs=pl.BlockSpec((spb, tm, cb), lambda i, j: (j, i, 0)),
        out_shape=jax.ShapeDtypeStruct((n // cb, m, cb), out_dtype),
        compiler_params=_compiler_params(("parallel", "parallel")),
        name="matmul_slabs",
    )(a, w)


def _mm_glu_kernel(*refs, nslab):
    a_refs, (bv_ref, bg_ref, o_ref) = refs[:nslab], refs[nslab:]
    a = jnp.concatenate([r[...] for r in a_refs], axis=1)
    for sl in _col_slices(o_ref.shape[1]):
        v = _dot(a, bv_ref[:, sl])
        g = _dot(a, bg_ref[:, sl])
        o_ref[:, sl] = (v * _sigmoid(g)).astype(o_ref.dtype)


def _matmul_glu(a_slabs, w, out_dtype, tm=1024, tn=1024):
    nslab, m, cb = a_slabs.shape
    k = nslab * cb
    n = w.shape[1] // 2
    tm, tn = _pick_tile(m, tm), _pick_tile(n, tn)
    nj = n // tn
    slab_specs = [pl.BlockSpec((None, tm, cb), functools.partial(lambda i, j, sidx: (sidx, i, 0), sidx=sidx))
                  for sidx in range(nslab)]
    return pl.pallas_call(
        functools.partial(_mm_glu_kernel, nslab=nslab),
        grid=(m // tm, nj),
        in_specs=slab_specs + [pl.BlockSpec((k, tn), lambda i, j: (0, j)),
                               pl.BlockSpec((k, tn), lambda i, j: (0, j + nj))],
        out_specs=pl.BlockSpec((tm, tn), lambda i, j: (i, j)),
        out_shape=jax.ShapeDtypeStruct((m, n), out_dtype),
        compiler_params=_compiler_params(("parallel", "parallel")),
        name="matmul_glu",
    )(*([a_slabs] * nslab), w, w)


def _mm_mix_kernel(a_ref, b_ref, g1_ref, g2_ref, y2_ref, o_ref):
    a = a_ref[...]
    for sl in _col_slices(o_ref.shape[1]):
        y1 = _dot(a, b_ref[:, sl])
        o_ref[:, sl] = (_sigmoid(g1_ref[:, sl].astype(F32)) * y1
                        + _sigmoid(g2_ref[:, sl].astype(F32)) * y2_ref[:, sl].astype(F32)
                        ).astype(o_ref.dtype)


def _matmul_mix(a, w, proj, gate_col, y2, out_dtype, tm=1024):
    m, k = a.shape
    n = w.shape[1]
    tm = _pick_tile(m, tm)
    tn = math.gcd(math.gcd(512, gate_col), n)
    nj = n // tn
    g0 = gate_col // tn
    return pl.pallas_call(
        _mm_mix_kernel,
        grid=(m // tm, nj),
        in_specs=[pl.BlockSpec((tm, k), lambda i, j: (i, 0)),
                  pl.BlockSpec((k, tn), lambda i, j: (0, j)),
                  pl.BlockSpec((tm, tn), lambda i, j: (i, g0 + j)),
                  pl.BlockSpec((tm, tn), lambda i, j: (i, g0 + nj + j)),
                  pl.BlockSpec((tm, tn), lambda i, j: (i, j))],
        out_specs=pl.BlockSpec((tm, tn), lambda i, j: (i, j)),
        out_shape=jax.ShapeDtypeStruct((m, n), out_dtype),
        compiler_params=_compiler_params(("parallel", "parallel")),
        name="matmul_mix",
    )(a, w, proj, proj, y2)


def _mm_resid_kernel(a_ref, b_ref, x_ref, g_ref, o_ref, *scratch, nk):
    if nk == 1:
        a = a_ref[...]
        for sl in _col_slices(o_ref.shape[1]):
            o_ref[:, sl] = x_ref[:, sl] + g_ref[:, sl] * _dot(a, b_ref[:, sl])
        return
    acc_ref, = scratch
    k = pl.program_id(2)

    @pl.when(k == 0)
    def _():
        acc_ref[...] = jnp.zeros_like(acc_ref)

    acc_ref[...] += _dot(a_ref[...], b_ref[...])

    @pl.when(k == nk - 1)
    def _():
        o_ref[...] = x_ref[...] + g_ref[...] * acc_ref[...]


def _matmul_resid(a, w, x, gate, tm=1024, tk=4096):
    m, k = a.shape
    n = w.shape[1]
    tk, nk = _k_tiling(k, tk)
    tm, tn = _pick_tile(m, tm), _pick_tile(n, 1024 if nk > 1 else 512)
    scratch = [pltpu.VMEM((tm, tn), F32)] if nk > 1 else []
    return pl.pallas_call(
        functools.partial(_mm_resid_kernel, nk=nk),
        grid=(m // tm, n // tn, nk),
        in_specs=[pl.BlockSpec((tm, tk), lambda i, j, kk: (i, kk)),
                  pl.BlockSpec((tk, tn), lambda i, j, kk: (kk, j)),
                  pl.BlockSpec((tm, tn), lambda i, j, kk: (i, j)),
                  pl.BlockSpec((1, tn), lambda i, j, kk: (0, j))],
        out_specs=pl.BlockSpec((tm, tn), lambda i, j, kk: (i, j)),
        out_shape=jax.ShapeDtypeStruct((m, n), F32),
        scratch_shapes=scratch,
        compiler_params=_compiler_params(("parallel", "parallel", "arbitrary")),
        name="matmul_resid",
    )(a, w, x, gate.reshape(1, n))


def _conv_taps(buf_ref, w_ref, b_ref, base, rows, sl):
    k = w_ref.shape[0]
    out = b_ref[:, sl] + w_ref[k - 1:k, sl] * buf_ref[pl.ds(base, rows), sl]
    for j in range(1, k):
        out = out + w_ref[k - 1 - j:k - j, sl] * buf_ref[pl.ds(base - j, rows), sl]
    return out


def _mm_convgate_kernel(a_ref, ba_ref, bv_ref, wa_ref, wv_ref, ca_ref, cv_ref, o_ref,
                        ua_ref, uv_ref, halo_ref):
    i = pl.program_id(0)
    j = pl.program_id(1)
    tm = a_ref.shape[0]

    @pl.when(i == 0)
    def _():
        halo_ref[j] = jnp.zeros(halo_ref.shape[1:], F32)

    ua_ref[0:HALO, :] = halo_ref[j, 0]
    uv_ref[0:HALO, :] = halo_ref[j, 1]
    a = a_ref[...]
    for sl in _col_slices(o_ref.shape[1]):
        ua_ref[HALO:, sl] = _dot(a, ba_ref[:, sl])
        uv_ref[HALO:, sl] = _dot(a, bv_ref[:, sl])
        ca = _conv_taps(ua_ref, wa_ref, ca_ref, HALO, tm, sl)
        cv = _conv_taps(uv_ref, wv_ref, cv_ref, HALO, tm, sl)
        o_ref[:, sl] = (ca * _sigmoid(ca) * cv).astype(o_ref.dtype)
    halo_ref[j, 0] = ua_ref[tm:tm + HALO, :]
    halo_ref[j, 1] = uv_ref[tm:tm + HALO, :]


def _matmul_convgate(a, w, conv_w, conv_b, out_dtype, tm=1024):
    m, k = a.shape
    n = w.shape[1] // 2
    kc = conv_w.shape[0]
    assert kc - 1 <= HALO
    tm = _pick_tile(m, tm)
    tn = math.gcd(512, n)
    nj = n // tn
    conv_b = conv_b.reshape(1, 2 * n)
    return pl.pallas_call(
        _mm_convgate_kernel,
        grid=(m // tm, nj),
        in_specs=[pl.BlockSpec((tm, k), lambda i, j: (i, 0)),
                  pl.BlockSpec((k, tn), lambda i, j: (0, j)),
                  pl.BlockSpec((k, tn), lambda i, j: (0, j + nj)),
                  pl.BlockSpec((kc, tn), lambda i, j: (0, j)),
                  pl.BlockSpec((kc, tn), lambda i, j: (0, j + nj)),
                  pl.BlockSpec((1, tn), lambda i, j: (0, j)),
                  pl.BlockSpec((1, tn), lambda i, j: (0, j + nj))],
        out_specs=pl.BlockSpec((tm, tn), lambda i, j: (i, j)),
        out_shape=jax.ShapeDtypeStruct((m, n), out_dtype),
        scratch_shapes=[pltpu.VMEM((tm + HALO, tn), F32), pltpu.VMEM((tm + HALO, tn), F32),
                        pltpu.VMEM((nj, 2, HALO, tn), F32)],
        compiler_params=_compiler_params(("arbitrary", "arbitrary")),
        name="matmul_convgate",
    )(a, w, w, conv_w, conv_w, conv_b, conv_b)


def _norm_kernel(x_ref, w_ref, s_ref, o_ref):
    x = x_ref[...]
    ms = jnp.mean(x * x, axis=-1, keepdims=True)
    o_ref[...] = (x * lax.rsqrt(ms + EPS) * w_ref[...] + s_ref[...]).astype(o_ref.dtype)


def _norm_mod(x, w, shift, out_dtype, tr=256):
    s, d = x.shape
    tr = _pick_tile(s, tr)
    return pl.pallas_call(
        _norm_kernel,
        grid=(s // tr,),
        in_specs=[pl.BlockSpec((tr, d), lambda i: (i, 0)),
                  pl.BlockSpec((1, d), lambda i: (0, 0)),
                  pl.BlockSpec((1, d), lambda i: (0, 0))],
        out_specs=pl.BlockSpec((tr, d), lambda i: (i, 0)),
        out_shape=jax.ShapeDtypeStruct((s, d), out_dtype),
        compiler_params=_compiler_params(("parallel",)),
        name="norm_mod",
    )(x, w.reshape(1, d), shift.reshape(1, d))


def _split3(x):
    hi = x.astype(BF16)
    r1 = x - hi.astype(F32)
    mid = r1.astype(BF16)
    lo = (r1 - mid.astype(F32)).astype(BF16)
    return hi, mid, lo


def _dot01_left(t01, x):
    hi, mid, lo = _split3(x)
    return _dot(t01, hi) + _dot(t01, mid) + _dot(t01, lo)


def _dot01_right(x, t01):
    hi, mid, lo = _split3(x)
    return _dot(hi, t01) + _dot(mid, t01) + _dot(lo, t01)


def _softplus(x):
    return jnp.maximum(x, 0.0) + jnp.log1p(jnp.exp(-jnp.abs(x)))


def _ssd_kernel(x_ref, b_ref, c_ref, z_ref, dt_ref, dtT_ref, bias_ref, biasT_ref,
                wx_ref, wb_ref, wc_ref, cbx_ref, cbb_ref, cbc_ref,
                arow_ref, acol_ref, e_ref, dexp_ref, nw_ref,
                o_ref, state_ref, xbuf_ref, bbuf_ref, cbuf_ref, *, nchunk, hpg):
    q = CHUNK
    p = SSD_HEAD_DIM
    r = x_ref.shape[0]

    @pl.when(pl.program_id(1) == 0)
    def _():
        state_ref[...] = jnp.zeros_like(state_ref)
        xbuf_ref[0:HALO, :] = jnp.zeros((HALO, xbuf_ref.shape[1]), F32)
        bbuf_ref[0:HALO, :] = jnp.zeros((HALO, bbuf_ref.shape[1]), F32)
        cbuf_ref[0:HALO, :] = jnp.zeros((HALO, cbuf_ref.shape[1]), F32)

    xbuf_ref[HALO:, :] = x_ref[...].astype(F32)
    bbuf_ref[HALO:, :] = b_ref[...].astype(F32)
    cbuf_ref[HALO:, :] = c_ref[...].astype(F32)

    row = lax.broadcasted_iota(jnp.int32, (q, q), 0)
    col = lax.broadcasted_iota(jnp.int32, (q, q), 1)
    mask = row >= col
    tri = mask.astype(BF16)
    tri_t = (row <= col).astype(BF16)
    e = e_ref[...]
    dexp = dexp_ref[...]
    acol = acol_ref[...]
    nw = nw_ref[...]
    full = slice(None)

    def conv_silu(buf_ref, w_ref, cb_ref, base):
        v = _conv_taps(buf_ref, w_ref, cb_ref, base, q, full)
        return v * _sigmoid(v)

    for ci in range(nchunk):
        sl = pl.ds(ci * q, q)
        base = HALO + ci * q
        x = conv_silu(xbuf_ref, wx_ref, cbx_ref, base)
        b = conv_silu(bbuf_ref, wb_ref, cbb_ref, base).astype(BF16)
        c = conv_silu(cbuf_ref, wc_ref, cbc_ref, base).astype(BF16)
        dt = _softplus(dt_ref[sl, :] + bias_ref[...])
        dt_t = _softplus(dtT_ref[:, sl] + biasT_ref[...])
        dt_exp = _dot01_right(dt, e)
        cum_h = _dot01_left(tri, dt * arow_ref[...])
        cum = _dot01_right(cum_h, e)
        cum_t = _dot01_right(dt_t * acol, tri_t)
        cum_last = cum[q - 1:q, :]
        xdt = x * dt_exp
        cb = lax.dot_general(c, b, (((1,), (1,)), ((), ())), preferred_element_type=F32)
        st = state_ref[...]
        y_inter = _dot(c, st.astype(BF16)) * jnp.exp(cum)
        ys = []
        for h in range(hpg):
            seg = cum[:, h * p:h * p + 1] - cum_t[h:h + 1, :]
            dec = jnp.exp(jnp.where(mask, seg, -jnp.inf))
            w = (cb * dec).astype(BF16)
            ys.append(_dot(w, xdt[:, h * p:(h + 1) * p].astype(BF16)))
        y = jnp.concatenate(ys, axis=1) + y_inter + dexp * x
        dec_end = jnp.exp(cum_last - cum)
        b_t = b.astype(F32).T.astype(BF16)
        state_ref[...] = st * jnp.exp(cum_last) + _dot(b_t, (xdt * dec_end).astype(BF16))
        zf = z_ref[sl, :].astype(F32)
        gt = y * (zf * _sigmoid(zf))
        ms = jnp.mean(gt * gt, axis=-1, keepdims=True)
        o_ref[sl, :] = (gt * lax.rsqrt(ms + EPS) * nw).astype(o_ref.dtype)

    xbuf_ref[0:HALO, :] = xbuf_ref[r:r + HALO, :]
    bbuf_ref[0:HALO, :] = bbuf_ref[r:r + HALO, :]
    cbuf_ref[0:HALO, :] = cbuf_ref[r:r + HALO, :]


def _ssd_branch(proj, dt_raw, conv_w, conv_b, dt_bias, a_log, d, norm_w, inner, rows_per_step=512):
    s = proj.shape[0]
    h = dt_bias.shape[0]
    g = SSD_GROUPS
    hpg = h // g
    gw = hpg * SSD_HEAD_DIM
    n = SSD_STATE
    r = _pick_tile(s, rows_per_step)
    kc = conv_w.shape[0]
    assert r % CHUNK == 0 and inner == g * gw and gw % LANES == 0 and h <= LANES and kc - 1 <= HALO
    a = -jnp.exp(a_log)
    dt_t = dt_raw[:, :h].T.reshape(g, hpg, s)
    bias = jnp.pad(dt_bias, (0, LANES - h)).reshape(1, LANES)
    bias_t = jnp.broadcast_to(dt_bias.reshape(g, hpg, 1), (g, hpg, CHUNK))
    arow = jnp.pad(a, (0, LANES - h)).reshape(1, LANES)
    dexp = jnp.repeat(d, SSD_HEAD_DIM).reshape(g, 1, gw)
    acol = jnp.broadcast_to(a.reshape(g, hpg, 1), (g, hpg, CHUNK))
    lane_head = jnp.arange(inner) // SSD_HEAD_DIM
    e = (jnp.arange(LANES)[:, None] == lane_head[None, :]).astype(BF16)
    e = e.reshape(LANES, g, gw).transpose(1, 0, 2)
    conv_b = conv_b.reshape(1, -1)
    xb0 = inner // gw
    bb0 = 2 * inner // n
    cb0 = bb0 + g
    wb0 = inner // n
    wc0 = wb0 + g
    per_g = lambda gi, t: (gi, 0, 0)
    return pl.pallas_call(
        functools.partial(_ssd_kernel, nchunk=r // CHUNK, hpg=hpg),
        grid=(g, s // r),
        in_specs=[pl.BlockSpec((r, gw), lambda gi, t: (t, xb0 + gi)),
                  pl.BlockSpec((r, n), lambda gi, t: (t, bb0 + gi)),
                  pl.BlockSpec((r, n), lambda gi, t: (t, cb0 + gi)),
                  pl.BlockSpec((r, gw), lambda gi, t: (t, gi)),
                  pl.BlockSpec((r, LANES), lambda gi, t: (t, 0)),
                  pl.BlockSpec((None, hpg, r), lambda gi, t: (gi, 0, t)),
                  pl.BlockSpec((1, LANES), lambda gi, t: (0, 0)),
                  pl.BlockSpec((None, hpg, CHUNK), per_g),
                  pl.BlockSpec((kc, gw), lambda gi, t: (0, gi)),
                  pl.BlockSpec((kc, n), lambda gi, t: (0, wb0 + gi)),
                  pl.BlockSpec((kc, n), lambda gi, t: (0, wc0 + gi)),
                  pl.BlockSpec((1, gw), lambda gi, t: (0, gi)),
                  pl.BlockSpec((1, n), lambda gi, t: (0, wb0 + gi)),
                  pl.BlockSpec((1, n), lambda gi, t: (0, wc0 + gi)),
                  pl.BlockSpec((1, LANES), lambda gi, t: (0, 0)),
                  pl.BlockSpec((None, hpg, CHUNK), per_g),
                  pl.BlockSpec((None, LANES, gw), per_g),
                  pl.BlockSpec((None, 1, gw), per_g),
                  pl.BlockSpec((1, gw), lambda gi, t: (0, gi))],
        out_specs=pl.BlockSpec((r, gw), lambda gi, t: (t, gi)),
        out_shape=jax.ShapeDtypeStruct((s, inner), BF16),
        scratch_shapes=[pltpu.VMEM((n, gw), F32), pltpu.VMEM((HALO + r, gw), F32),
                        pltpu.VMEM((HALO + r, n), F32), pltpu.VMEM((HALO + r, n), F32)],
        compiler_params=_compiler_params(("parallel", "arbitrary")),
        name="ssd_branch",
    )(proj, proj, proj, proj, dt_raw, dt_t, bias, bias_t, conv_w, conv_w, conv_w,
      conv_b, conv_b, conv_b, arow, acol, e, dexp, norm_w.reshape(1, inner))


S5_BLOCK = 8


def _cmul(ar, ai, br, bi):
    return ar * br - ai * bi, ar * bi + ai * br


def _s5_kernel(u_ref, t_ref, wb_ref, wc_ref, k_ref, d_ref, o_ref, h_ref, carry_ref):
    m, rcb = u_ref.shape
    cb = t_ref.shape[1]
    nr = rcb // cb
    w = h_ref.shape[1] // 2
    ns = SUBLANES

    @pl.when(pl.program_id(1) == 0)
    def _():
        carry_ref[...] = jnp.zeros_like(carry_ref)

    h_ref[...] = _dot(u_ref[...], wb_ref[...])
    first_row = lax.broadcasted_iota(jnp.int32, (ns, w), 0) == 0

    def scan(r, carry):
        cr, ci = carry
        rows = pl.ds(pl.multiple_of(r * ns, ns), ns)
        xr = h_ref[rows, :w]
        xi = h_ref[rows, w:]
        for idx, shift in enumerate((1, 2, 4)):
            pr, pi = _cmul(k_ref[2 * idx], k_ref[2 * idx + 1],
                           pltpu.roll(xr, shift, 0), pltpu.roll(xi, shift, 0))
            xr, xi = xr + pr, xi + pi
        pr, pi = _cmul(k_ref[6], k_ref[7], cr, ci)
        xr, xi = xr + pr, xi + pi
        h_ref[rows, :w] = jnp.where(first_row, cr, pltpu.roll(xr, 1, 0))
        h_ref[rows, w:] = jnp.where(first_row, ci, pltpu.roll(xi, 1, 0))
        return xr[ns - 1:ns, :], xi[ns - 1:ns, :]

    cr, ci = lax.fori_loop(0, m // ns, scan, (carry_ref[0:1, :w], carry_ref[0:1, w:]))
    carry_ref[0:1, :w] = cr
    carry_ref[0:1, w:] = ci

    y_state = _dot(h_ref[...].astype(BF16), wc_ref[...])
    for qo in range(nr):
        lo = slice(qo * cb, (qo + 1) * cb)
        acc = y_state[:, lo] + d_ref[:, lo] * u_ref[:, lo].astype(F32)
        for qi in range(qo + 1):
            acc = acc + _dot(u_ref[:, qi * cb:(qi + 1) * cb], t_ref[qo - qi])
        o_ref[:, lo] = jax.nn.gelu(acc).astype(o_ref.dtype)


def _s5_constants(lam_re, lam_im, log_dt, b_re, b_im, c_re, c_im):
    hi = lax.Precision.HIGHEST
    g5, n5 = lam_re.shape
    nr = S5_BLOCK
    gpb = min(S5_GROUPS_PER_BLOCK, g5)
    nblk = g5 // gpb
    dt = jnp.exp(log_dt)[:, None]
    mag = jnp.exp(lam_re * dt)
    ang = lam_im * dt
    abar_re, abar_im = mag * jnp.cos(ang), mag * jnp.sin(ang)
    den = lam_re * lam_re + lam_im * lam_im
    nre, nim = abar_re - 1.0, abar_im
    coef_re = (nre * lam_re + nim * lam_im) / den
    coef_im = (nim * lam_re - nre * lam_im) / den
    bbar_re = coef_re[..., None] * b_re - coef_im[..., None] * b_im
    bbar_im = coef_re[..., None] * b_im + coef_im[..., None] * b_re

    def powers(pr, pi, count):
        out = [(jnp.ones_like(pr), jnp.zeros_like(pi))]
        for _ in range(count):
            out.append(_cmul(out[-1][0], out[-1][1], pr, pi))
        return out

    apow = powers(abar_re, abar_im, nr)
    bpow = powers(apow[nr][0], apow[nr][1], SUBLANES)

    eye = jnp.eye(gpb, dtype=F32)

    def bdiag(v, pattern):
        v = v.reshape((nblk, gpb) + v.shape[1:])
        m = jnp.einsum(pattern, v, eye)
        return m.reshape(nblk, m.shape[1] * m.shape[2], m.shape[3] * m.shape[4])

    taps = []
    for d in range(nr):
        ar, ai = apow[d]
        mr, mi = _cmul(ar[..., None], ai[..., None], bbar_re, bbar_im)
        kd = (jnp.einsum('gon,gni->goi', c_re, mr, precision=hi)
              - jnp.einsum('gon,gni->goi', c_im, mi, precision=hi))
        taps.append(bdiag(kd, 'kgoi,gh->kgiho'))
    ttab = jnp.stack(taps, axis=1).astype(BF16)

    wb_rows, wc_cols = [], []
    for q in range(nr):
        ar, ai = apow[nr - 1 - q]
        mr, mi = _cmul(ar[..., None], ai[..., None], bbar_re, bbar_im)
        wb_rows.append(jnp.concatenate([bdiag(mr, 'kgni,gh->kgihn'), bdiag(mi, 'kgni,gh->kgihn')], axis=2))
        ar, ai = apow[q + 1]
        mr, mi = _cmul(c_re, c_im, ar[:, None, :], ai[:, None, :])
        wc_cols.append(jnp.concatenate([bdiag(mr, 'kgin,gh->kgnhi'), -bdiag(mi, 'kgin,gh->kgnhi')], axis=1))
    wb = jnp.concatenate(wb_rows, axis=1).astype(BF16)
    wc = jnp.concatenate(wc_cols, axis=2).astype(BF16)

    def lanes(v):
        return v.reshape(nblk, 1, gpb * n5)

    rows = jnp.arange(SUBLANES)[:, None]
    kinds = []
    for shift in (1, 2, 4):
        for part in bpow[shift]:
            kinds.append(jnp.where(rows >= shift, lanes(part), 0.0))
    for comp in (0, 1):
        kinds.append(jnp.concatenate([lanes(bpow[r + 1][comp]) for r in range(SUBLANES)], axis=1))
    kconst = jnp.stack(kinds, axis=1)
    return ttab, wb, wc, kconst


def _s5_branch(u, params, d, blocks_per_step=512):
    nblk, s, cb = u.shape
    nr = S5_BLOCK
    ttab, wb, wc, kconst = _s5_constants(*params)
    w2 = wb.shape[2]
    assert s % nr == 0 and ttab.shape == (nblk, nr, cb, cb)
    m = _pick_tile(s // nr, blocks_per_step)
    u2 = u.reshape(nblk, s // nr, nr * cb)
    d2 = jnp.tile(d.reshape(nblk, 1, cb), (1, 1, nr))
    per_b = lambda gb, t: (gb, 0, 0)
    out = pl.pallas_call(
        _s5_kernel,
        grid=(nblk, s // nr // m),
        in_specs=[pl.BlockSpec((None, m, nr * cb), lambda gb, t: (gb, t, 0)),
                  pl.BlockSpec((None, nr, cb, cb), lambda gb, t: (gb, 0, 0, 0)),
                  pl.BlockSpec((None, nr * cb, w2), per_b),
                  pl.BlockSpec((None, w2, nr * cb), per_b),
                  pl.BlockSpec((None, 8, SUBLANES, w2 // 2), lambda gb, t: (gb, 0, 0, 0)),
                  pl.BlockSpec((None, 1, nr * cb), per_b)],
        out_specs=pl.BlockSpec((None, m, nr * cb), lambda gb, t: (gb, t, 0)),
        out_shape=jax.ShapeDtypeStruct((nblk, s // nr, nr * cb), BF16),
        scratch_shapes=[pltpu.VMEM((m, w2), F32), pltpu.VMEM((SUBLANES, w2), F32)],
        compiler_params=_compiler_params(("parallel", "arbitrary")),
        name="s5_branch",
    )(u2, ttab, wb, wc, kconst, d2)
    return out.reshape(nblk, s, cb)


def kernel(x, c, ada_w1, ada_w2, ada_b, ada_table, norm_mix_w, w_in, ssd_conv_w, ssd_conv_b,
           ssd_dt_bias, ssd_a_log, ssd_d, ssd_norm_w, w_ssd_out, s5_lam_re, s5_lam_im, s5_log_dt,
           s5_b_re, s5_b_im, s5_c_re, s5_c_im, s5_d, w_glu, w_out, norm_ffn_w, w_up, ffn_conv_w,
           ffn_conv_b, w_down, final_norm_w):
    bsz, seq, dm = x.shape
    assert bsz == 1
    depth = w_in.shape[0]
    heads = ssd_dt_bias.shape[1]
    inner = heads * SSD_HEAD_DIM
    xbc_w = inner + 2 * SSD_GROUPS * SSD_STATE
    w5 = s5_d.shape[1]
    o_dt = inner + xbc_w
    o_u5 = o_dt + heads
    s5_cb = min(S5_GROUPS_PER_BLOCK, w5 // S5_GROUP_CH) * S5_GROUP_CH

    cpad = jnp.pad(jax.nn.silu(c), ((0, 2 * SUBLANES - bsz), (0, 0))).astype(BF16)
    t1 = _matmul(cpad, ada_w1.astype(BF16), BF16)
    mod = (_matmul(t1, ada_w2.astype(BF16), F32)[0] + ada_b).reshape(N_MOD, dm)

    xs = x[0]
    for l in range(depth):
        m = mod + ada_table[l]
        h = _norm_mod(xs, norm_mix_w[l] * (1.0 + m[1]), m[0], BF16)
        wl = w_in[l]
        w_main = jnp.concatenate([wl[:, :o_dt], wl[:, o_u5 + w5:]], axis=1).astype(BF16)
        w_dt = jnp.pad(wl[:, o_dt:o_u5], ((0, 0), (0, LANES - heads))).astype(BF16)
        proj = _matmul(h, w_main, BF16)
        u5 = _matmul_slabs(h, wl[:, o_u5:o_u5 + w5].astype(BF16), s5_cb, BF16)
        dt_raw = _matmul(h, w_dt, F32)
        g_ssd = _ssd_branch(proj, dt_raw, ssd_conv_w[l], ssd_conv_b[l], ssd_dt_bias[l],
                            ssd_a_log[l], ssd_d[l], ssd_norm_w[l], inner)
        s5_params = (s5_lam_re[l], s5_lam_im[l], s5_log_dt[l], s5_b_re[l], s5_b_im[l],
                     s5_c_re[l], s5_c_im[l])
        g_s5 = _s5_branch(u5, s5_params, s5_d[l])
        y_s5 = _matmul_glu(g_s5, w_glu[l].astype(BF16), BF16)
        mix = _matmul_mix(g_ssd, w_ssd_out[l].astype(BF16), proj, o_dt, y_s5, BF16)
        xs = _matmul_resid(mix, w_out[l].astype(BF16), xs, m[2])
        h = _norm_mod(xs, norm_ffn_w[l] * (1.0 + m[4]), m[3], BF16)
        act = _matmul_convgate(h, w_up[l].astype(BF16), ffn_conv_w[l], ffn_conv_b[l], BF16)
        xs = _matmul_resid(act, w_down[l].astype(BF16), xs, m[5])
    out = _norm_mod(xs, final_norm_w, jnp.zeros_like(final_norm_w), F32)
    return out[None]
```

```python
import functools
import math

import jax
import jax.numpy as jnp
from jax import lax
from jax.experimental import pallas as pl
from jax.experimental.pallas import tpu as pltpu

F32 = jnp.float32
BF16 = jnp.bfloat16

SSD_HEAD_DIM = 64
SSD_GROUPS = 8
SSD_STATE = 128
CHUNK = 128
S5_GROUP_CH = 16
N_MOD = 6
EPS = 1e-6

V7X_VMEM_BYTES = 64 * 1024 * 1024
VMEM_LIMIT_BYTES = V7X_VMEM_BYTES - 8 * 1024 * 1024
SUBLANES = 8
LANES = 128
MXU_DIM = 256
S5_GROUPS_PER_BLOCK = MXU_DIM // S5_GROUP_CH
HALO = SUBLANES


def _compiler_params(semantics):
    return pltpu.CompilerParams(dimension_semantics=semantics,
                                vmem_limit_bytes=VMEM_LIMIT_BYTES)


def _pick_tile(dim, want):
    t = min(dim, want)
    while dim % t:
        t //= 2
    return t


def _dot(a, b):
    return jnp.dot(a, b, preferred_element_type=F32)


def _sigmoid(x):
    return jax.nn.sigmoid(x)


def _mm_kernel(a_ref, b_ref, o_ref, *scratch, nk):
    if nk == 1:
        o_ref[...] = _dot(a_ref[...], b_ref[...]).astype(o_ref.dtype)
        return
    acc_ref, = scratch
    k = pl.program_id(2)

    @pl.when(k == 0)
    def _():
        acc_ref[...] = jnp.zeros_like(acc_ref)

    acc_ref[...] += _dot(a_ref[...], b_ref[...])

    @pl.when(k == nk - 1)
    def _():
        o_ref[...] = acc_ref[...].astype(o_ref.dtype)


def _k_tiling(k, tk):
    if k % tk:
        tk = k // 2 if k > tk else k
    tk = min(tk, k)
    assert k % tk == 0
    return tk, k // tk


def _matmul(a, b, out_dtype, tm=1024, tn=1024, tk=4096):
    m, k = a.shape
    n = b.shape[1]
    tm, tn = _pick_tile(m, tm), _pick_tile(n, tn)
    tk, nk = _k_tiling(k, tk)
    scratch = [pltpu.VMEM((tm, tn), F32)] if nk > 1 else []
    return pl.pallas_call(
        functools.partial(_mm_kernel, nk=nk),
        grid=(m // tm, n // tn, nk),
        in_specs=[pl.BlockSpec((tm, tk), lambda i, j, kk: (i, kk)),
                  pl.BlockSpec((tk, tn), lambda i, j, kk: (kk, j))],
        out_specs=pl.BlockSpec((tm, tn), lambda i, j, kk: (i, j)),
        out_shape=jax.ShapeDtypeStruct((m, n), out_dtype),
        scratch_shapes=scratch,
        compiler_params=_compiler_params(("parallel", "parallel", "arbitrary")),
        name="matmul",
    )(a, b)


def _col_slices(tn):
    cs = min(tn, MXU_DIM)
    return [slice(c, c + cs) for c in range(0, tn, cs)]


def _mm_slab_kernel(a_ref, b_ref, o_ref):
    acc = _dot(a_ref[...], b_ref[...])
    cb = o_ref.shape[2]
    for sidx in range(o_ref.shape[0]):
        o_ref[sidx] = acc[:, sidx * cb:(sidx + 1) * cb].astype(o_ref.dtype)


def _matmul_slabs(a, w, cb, out_dtype, tm=1024, tn=1024):
    m, k = a.shape
    n = w.shape[1]
    tm, tn = _pick_tile(m, tm), _pick_tile(n, tn)
    assert tn % cb == 0
    spb = tn // cb
    return pl.pallas_call(
        _mm_slab_kernel,
        grid=(m // tm, n // tn),
        in_specs=[pl.BlockSpec((tm, k), lambda i, j: (i, 0)),
                  pl.BlockSpec((k, tn), lambda i, j: (0, j))],
        out_specs=pl.BlockSpec((spb, tm, cb), lambda i, j: (j, i, 0)),
        out_shape=jax.ShapeDtypeStruct((n // cb, m, cb), out_dtype),
        compiler_params=_compiler_params(("parallel", "parallel")),
        name="matmul_slabs",
    )(a, w)


def _mm_glu_kernel(*refs, nslab):
    a_refs, (bv_ref, bg_ref, o_ref) = refs[:nslab], refs[nslab:]
    a = jnp.concatenate([r[...] for r in a_refs], axis=1)
    for sl in _col_slices(o_ref.shape[1]):
        v = _dot(a, bv_ref[:, sl])
        g = _dot(a, bg_ref[:, sl])
        o_ref[:, sl] = (v * _sigmoid(g)).astype(o_ref.dtype)


def _matmul_glu(a_slabs, w, out_dtype, tm=1024, tn=1024):
    nslab, m, cb = a_slabs.shape
    k = nslab * cb
    n = w.shape[1] // 2
    tm, tn = _pick_tile(m, tm), _pick_tile(n, tn)
    nj = n // tn
    slab_specs = [pl.BlockSpec((None, tm, cb), functools.partial(lambda i, j, sidx: (sidx, i, 0), sidx=sidx))
                  for sidx in range(nslab)]
    return pl.pallas_call(
        functools.partial(_mm_glu_kernel, nslab=nslab),
        grid=(m // tm, nj),
        in_specs=slab_specs + [pl.BlockSpec((k, tn), lambda i, j: (0, j)),
                               pl.BlockSpec((k, tn), lambda i, j: (0, j + nj))],
        out_specs=pl.BlockSpec((tm, tn), lambda i, j: (i, j)),
        out_shape=jax.ShapeDtypeStruct((m, n), out_dtype),
        compiler_params=_compiler_params(("parallel", "parallel")),
        name="matmul_glu",
    )(*([a_slabs] * nslab), w, w)


def _mm_mix_kernel(a_ref, b_ref, g1_ref, g2_ref, y2_ref, o_ref):
    a = a_ref[...]
    for sl in _col_slices(o_ref.shape[1]):
        y1 = _dot(a, b_ref[:, sl])
        o_ref[:, sl] = (_sigmoid(g1_ref[:, sl].astype(F32)) * y1
                        + _sigmoid(g2_ref[:, sl].astype(F32)) * y2_ref[:, sl].astype(F32)
                        ).astype(o_ref.dtype)


def _matmul_mix(a, w, proj, gate_col, y2, out_dtype, tm=1024):
    m, k = a.shape
    n = w.shape[1]
    tm = _pick_tile(m, tm)
    tn = math.gcd(math.gcd(512, gate_col), n)
    nj = n // tn
    g0 = gate_col // tn
    return pl.pallas_call(
        _mm_mix_kernel,
        grid=(m // tm, nj),
        in_specs=[pl.BlockSpec((tm, k), lambda i, j: (i, 0)),
                  pl.BlockSpec((k, tn), lambda i, j: (0, j)),
                  pl.BlockSpec((tm, tn), lambda i, j: (i, g0 + j)),
                  pl.BlockSpec((tm, tn), lambda i, j: (i, g0 + nj + j)),
                  pl.BlockSpec((tm, tn), lambda i, j: (i, j))],
        out_specs=pl.BlockSpec((tm, tn), lambda i, j: (i, j)),
        out_shape=jax.ShapeDtypeStruct((m, n), out_dtype),
        compiler_params=_compiler_params(("parallel", "parallel")),
        name="matmul_mix",
    )(a, w, proj, proj, y2)


def _mm_resid_kernel(a_ref, b_ref, x_ref, g_ref, o_ref, *scratch, nk):
    if nk == 1:
        a = a_ref[...]
        for sl in _col_slices(o_ref.shape[1]):
            o_ref[:, sl] = x_ref[:, sl] + g_ref[:, sl] * _dot(a, b_ref[:, sl])
        return
    acc_ref, = scratch
    k = pl.program_id(2)

    @pl.when(k == 0)
    def _():
        acc_ref[...] = jnp.zeros_like(acc_ref)

    acc_ref[...] += _dot(a_ref[...], b_ref[...])

    @pl.when(k == nk - 1)
    def _():
        o_ref[...] = x_ref[...] + g_ref[...] * acc_ref[...]


def _matmul_resid(a, w, x, gate, tm=1024, tk=4096):
    m, k = a.shape
    n = w.shape[1]
    tk, nk = _k_tiling(k, tk)
    tm, tn = _pick_tile(m, tm), _pick_tile(n, 1024 if nk > 1 else 512)
    scratch = [pltpu.VMEM((tm, tn), F32)] if nk > 1 else []
    return pl.pallas_call(
        functools.partial(_mm_resid_kernel, nk=nk),
        grid=(m // tm, n // tn, nk),
        in_specs=[pl.BlockSpec((tm, tk), lambda i, j, kk: (i, kk)),
                  pl.BlockSpec((tk, tn), lambda i, j, kk: (kk, j)),
                  pl.BlockSpec((tm, tn), lambda i, j, kk: (i, j)),
                  pl.BlockSpec((1, tn), lambda i, j, kk: (0, j))],
        out_specs=pl.BlockSpec((tm, tn), lambda i, j, kk: (i, j)),
        out_shape=jax.ShapeDtypeStruct((m, n), F32),
        scratch_shapes=scratch,
        compiler_params=_compiler_params(("parallel", "parallel", "arbitrary")),
        name="matmul_resid",
    )(a, w, x, gate.reshape(1, n))


def _conv_taps(buf_ref, w_ref, b_ref, base, rows, sl):
    k = w_ref.shape[0]
    out = b_ref[:, sl] + w_ref[k - 1:k, sl] * buf_ref[pl.ds(base, rows), sl]
    for j in range(1, k):
        out = out + w_ref[k - 1 - j:k - j, sl] * buf_ref[pl.ds(base - j, rows), sl]
    return out


def _mm_convgate_kernel(a_ref, ba_ref, bv_ref, wa_ref, wv_ref, ca_ref, cv_ref, o_ref,
                        ua_ref, uv_ref, halo_ref):
    i = pl.program_id(0)
    j = pl.program_id(1)
    tm = a_ref.shape[0]

    @pl.when(i == 0)
    def _():
        halo_ref[j] = jnp.zeros(halo_ref.shape[1:], F32)

    ua_ref[0:HALO, :] = halo_ref[j, 0]
    uv_ref[0:HALO, :] = halo_ref[j, 1]
    a = a_ref[...]
    for sl in _col_slices(o_ref.shape[1]):
        ua_ref[HALO:, sl] = _dot(a, ba_ref[:, sl])
        uv_ref[HALO:, sl] = _dot(a, bv_ref[:, sl])
        ca = _conv_taps(ua_ref, wa_ref, ca_ref, HALO, tm, sl)
        cv = _conv_taps(uv_ref, wv_ref, cv_ref, HALO, tm, sl)
        o_ref[:, sl] = (ca * _sigmoid(ca) * cv).astype(o_ref.dtype)
    halo_ref[j, 0] = ua_ref[tm:tm + HALO, :]
    halo_ref[j, 1] = uv_ref[tm:tm + HALO, :]


def _matmul_convgate(a, w, conv_w, conv_b, out_dtype, tm=1024):
    m, k = a.shape
    n = w.shape[1] // 2
    kc = conv_w.shape[0]
    assert kc - 1 <= HALO
    tm = _pick_tile(m, tm)
    tn = math.gcd(512, n)
    nj = n // tn
    conv_b = conv_b.reshape(1, 2 * n)
    return pl.pallas_call(
        _mm_convgate_kernel,
        grid=(m // tm, nj),
        in_specs=[pl.BlockSpec((tm, k), lambda i, j: (i, 0)),
                  pl.BlockSpec((k, tn), lambda i, j: (0, j)),
                  pl.BlockSpec((k, tn), lambda i, j: (0, j + nj)),
                  pl.BlockSpec((kc, tn), lambda i, j: (0, j)),
                  pl.BlockSpec((kc, tn), lambda i, j: (0, j + nj)),
                  pl.BlockSpec((1, tn), lambda i, j: (0, j)),
                  pl.BlockSpec((1, tn), lambda i, j: (0, j + nj))],
        out_specs=pl.BlockSpec((tm, tn), lambda i, j: (i, j)),
        out_shape=jax.ShapeDtypeStruct((m, n), out_dtype),
        scratch_shapes=[pltpu.VMEM((tm + HALO, tn), F32), pltpu.VMEM((tm + HALO, tn), F32),
                        pltpu.VMEM((nj, 2, HALO, tn), F32)],
        compiler_params=_compiler_params(("arbitrary", "arbitrary")),
        name="matmul_convgate",
    )(a, w, w, conv_w, conv_w, conv_b, conv_b)


def _norm_kernel(x_ref, w_ref, s_ref, o_ref):
    x = x_ref[...]
    ms = jnp.mean(x * x, axis=-1, keepdims=True)
    o_ref[...] = (x * lax.rsqrt(ms + EPS) * w_ref[...] + s_ref[...]).astype(o_ref.dtype)


def _norm_mod(x, w, shift, out_dtype, tr=256):
    s, d = x.shape
    tr = _pick_tile(s, tr)
    return pl.pallas_call(
        _norm_kernel,
        grid=(s // tr,),
        in_specs=[pl.BlockSpec((tr, d), lambda i: (i, 0)),
                  pl.BlockSpec((1, d), lambda i: (0, 0)),
                  pl.BlockSpec((1, d), lambda i: (0, 0))],
        out_specs=pl.BlockSpec((tr, d), lambda i: (i, 0)),
        out_shape=jax.ShapeDtypeStruct((s, d), out_dtype),
        compiler_params=_compiler_params(("parallel",)),
        name="norm_mod",
    )(x, w.reshape(1, d), shift.reshape(1, d))


def _split3(x):
    hi = x.astype(BF16)
    r1 = x - hi.astype(F32)
    mid = r1.astype(BF16)
    lo = (r1 - mid.astype(F32)).astype(BF16)
    return hi, mid, lo


def _dot01_left(t01, x):
    hi, mid, lo = _split3(x)
    return _dot(t01, hi) + _dot(t01, mid) + _dot(t01, lo)


def _dot01_right(x, t01):
    hi, mid, lo = _split3(x)
    return _dot(hi, t01) + _dot(mid, t01) + _dot(lo, t01)


def _softplus(x):
    return jnp.maximum(x, 0.0) + jnp.log1p(jnp.exp(-jnp.abs(x)))


def _ssd_kernel(x_ref, b_ref, c_ref, z_ref, dt_ref, dtT_ref, bias_ref, biasT_ref,
                wx_ref, wb_ref, wc_ref, cbx_ref, cbb_ref, cbc_ref,
                arow_ref, acol_ref, e_ref, dexp_ref, nw_ref,
                o_ref, state_ref, xbuf_ref, bbuf_ref, cbuf_ref, *, nchunk, hpg):
    q = CHUNK
    p = SSD_HEAD_DIM
    r = x_ref.shape[0]

    @pl.when(pl.program_id(1) == 0)
    def _():
        state_ref[...] = jnp.zeros_like(state_ref)
        xbuf_ref[0:HALO, :] = jnp.zeros((HALO, xbuf_ref.shape[1]), F32)
        bbuf_ref[0:HALO, :] = jnp.zeros((HALO, bbuf_ref.shape[1]), F32)
        cbuf_ref[0:HALO, :] = jnp.zeros((HALO, cbuf_ref.shape[1]), F32)

    xbuf_ref[HALO:, :] = x_ref[...].astype(F32)
    bbuf_ref[HALO:, :] = b_ref[...].astype(F32)
    cbuf_ref[HALO:, :] = c_ref[...].astype(F32)

    row = lax.broadcasted_iota(jnp.int32, (q, q), 0)
    col = lax.broadcasted_iota(jnp.int32, (q, q), 1)
    mask = row >= col
    tri = mask.astype(BF16)
    tri_t = (row <= col).astype(BF16)
    e = e_ref[...]
    dexp = dexp_ref[...]
    acol = acol_ref[...]
    nw = nw_ref[...]
    full = slice(None)

    def conv_silu(buf_ref, w_ref, cb_ref, base):
        v = _conv_taps(buf_ref, w_ref, cb_ref, base, q, full)
        return v * _sigmoid(v)

    for ci in range(nchunk):
        sl = pl.ds(ci * q, q)
        base = HALO + ci * q
        x = conv_silu(xbuf_ref, wx_ref, cbx_ref, base)
        b = conv_silu(bbuf_ref, wb_ref, cbb_ref, base).astype(BF16)
        c = conv_silu(cbuf_ref, wc_ref, cbc_ref, base).astype(BF16)
        dt = _softplus(dt_ref[sl, :] + bias_ref[...])
        dt_t = _softplus(dtT_ref[:, sl] + biasT_ref[...])
        dt_exp = _dot01_right(dt, e)
        cum_h = _dot01_left(tri, dt * arow_ref[...])
        cum = _dot01_right(cum_h, e)
        cum_t = _dot01_right(dt_t * acol, tri_t)
        cum_last = cum[q - 1:q, :]
        xdt = x * dt_exp
        cb = lax.dot_general(c, b, (((1,), (1,)), ((), ())), preferred_element_type=F32)
        st = state_ref[...]
        y_inter = _dot(c, st.astype(BF16)) * jnp.exp(cum)
        ys = []
        for h in range(hpg):
            seg = cum[:, h * p:h * p + 1] - cum_t[h:h + 1, :]
            dec = jnp.exp(jnp.where(mask, seg, -jnp.inf))
            w = (cb * dec).astype(BF16)
            ys.append(_dot(w, xdt[:, h * p:(h + 1) * p].astype(BF16)))
        y = jnp.concatenate(ys, axis=1) + y_inter + dexp * x
        dec_end = jnp.exp(cum_last - cum)
        b_t = b.astype(F32).T.astype(BF16)
        state_ref[...] = st * jnp.exp(cum_last) + _dot(b_t, (xdt * dec_end).astype(BF16))
        zf = z_ref[sl, :].astype(F32)
        gt = y * (zf * _sigmoid(zf))
        ms = jnp.mean(gt * gt, axis=-1, keepdims=True)
        o_ref[sl, :] = (gt * lax.rsqrt(ms + EPS) * nw).astype(o_ref.dtype)

    xbuf_ref[0:HALO, :] = xbuf_ref[r:r + HALO, :]
    bbuf_ref[0:HALO, :] = bbuf_ref[r:r + HALO, :]
    cbuf_ref[0:HALO, :] = cbuf_ref[r:r + HALO, :]


def _ssd_branch(proj, dt_raw, conv_w, conv_b, dt_bias, a_log, d, norm_w, inner, rows_per_step=1024):
    s = proj.shape[0]
    h = dt_bias.shape[0]
    g = SSD_GROUPS
    hpg = h // g
    gw = hpg * SSD_HEAD_DIM
    n = SSD_STATE
    r = _pick_tile(s, rows_per_step)
    kc = conv_w.shape[0]
    assert r % CHUNK == 0 and inner == g * gw and gw % LANES == 0 and h <= LANES and kc - 1 <= HALO
    a = -jnp.exp(a_log)
    dt_t = dt_raw[:, :h].T.reshape(g, hpg, s)
    bias = jnp.pad(dt_bias, (0, LANES - h)).reshape(1, LANES)
    bias_t = jnp.broadcast_to(dt_bias.reshape(g, hpg, 1), (g, hpg, CHUNK))
    arow = jnp.pad(a, (0, LANES - h)).reshape(1, LANES)
    dexp = jnp.repeat(d, SSD_HEAD_DIM).reshape(g, 1, gw)
    acol = jnp.broadcast_to(a.reshape(g, hpg, 1), (g, hpg, CHUNK))
    lane_head = jnp.arange(inner) // SSD_HEAD_DIM
    e = (jnp.arange(LANES)[:, None] == lane_head[None, :]).astype(BF16)
    e = e.reshape(LANES, g, gw).transpose(1, 0, 2)
    conv_b = conv_b.reshape(1, -1)
    xb0 = inner // gw
    bb0 = 2 * inner // n
    cb0 = bb0 + g
    wb0 = inner // n
    wc0 = wb0 + g
    per_g = lambda gi, t: (gi, 0, 0)
    return pl.pallas_call(
        functools.partial(_ssd_kernel, nchunk=r // CHUNK, hpg=hpg),
        grid=(g, s // r),
        in_specs=[pl.BlockSpec((r, gw), lambda gi, t: (t, xb0 + gi)),
                  pl.BlockSpec((r, n), lambda gi, t: (t, bb0 + gi)),
                  pl.BlockSpec((r, n), lambda gi, t: (t, cb0 + gi)),
                  pl.BlockSpec((r, gw), lambda gi, t: (t, gi)),
                  pl.BlockSpec((r, LANES), lambda gi, t: (t, 0)),
                  pl.BlockSpec((None, hpg, r), lambda gi, t: (gi, 0, t)),
                  pl.BlockSpec((1, LANES), lambda gi, t: (0, 0)),
                  pl.BlockSpec((None, hpg, CHUNK), per_g),
                  pl.BlockSpec((kc, gw), lambda gi, t: (0, gi)),
                  pl.BlockSpec((kc, n), lambda gi, t: (0, wb0 + gi)),
                  pl.BlockSpec((kc, n), lambda gi, t: (0, wc0 + gi)),
                  pl.BlockSpec((1, gw), lambda gi, t: (0, gi)),
                  pl.BlockSpec((1, n), lambda gi, t: (0, wb0 + gi)),
                  pl.BlockSpec((1, n), lambda gi, t: (0, wc0 + gi)),
                  pl.BlockSpec((1, LANES), lambda gi, t: (0, 0)),
                  pl.BlockSpec((None, hpg, CHUNK), per_g),
                  pl.BlockSpec((None, LANES, gw), per_g),
                  pl.BlockSpec((None, 1, gw), per_g),
                  pl.BlockSpec((1, gw), lambda gi, t: (0, gi))],
        out_specs=pl.BlockSpec((r, gw), lambda gi, t: (t, gi)),
        out_shape=jax.ShapeDtypeStruct((s, inner), BF16),
        scratch_shapes=[pltpu.VMEM((n, gw), F32), pltpu.VMEM((HALO + r, gw), F32),
                        pltpu.VMEM((HALO + r, n), F32), pltpu.VMEM((HALO + r, n), F32)],
        compiler_params=_compiler_params(("parallel", "arbitrary")),
        name="ssd_branch",
    )(proj, proj, proj, proj, dt_raw, dt_t, bias, bias_t, conv_w, conv_w, conv_w,
      conv_b, conv_b, conv_b, arow, acol, e, dexp, norm_w.reshape(1, inner))


S5_BLOCK = 8


def _cmul(ar, ai, br, bi):
    return ar * br - ai * bi, ar * bi + ai * br


def _s5_kernel(u_ref, t_ref, wb_ref, wc_ref, k_ref, d_ref, o_ref, h_ref, carry_ref):
    m, rcb = u_ref.shape
    cb = t_ref.shape[1]
    nr = rcb // cb
    w = h_ref.shape[1] // 2
    ns = SUBLANES

    @pl.when(pl.program_id(1) == 0)
    def _():
        carry_ref[...] = jnp.zeros_like(carry_ref)

    h_ref[...] = _dot(u_ref[...], wb_ref[...])
    first_row = lax.broadcasted_iota(jnp.int32, (ns, w), 0) == 0

    def scan(r, carry):
        cr, ci = carry
        rows = pl.ds(pl.multiple_of(r * ns, ns), ns)
        xr = h_ref[rows, :w]
        xi = h_ref[rows, w:]
        for idx, shift in enumerate((1, 2, 4)):
            pr, pi = _cmul(k_ref[2 * idx], k_ref[2 * idx + 1],
                           pltpu.roll(xr, shift, 0), pltpu.roll(xi, shift, 0))
            xr, xi = xr + pr, xi + pi
        pr, pi = _cmul(k_ref[6], k_ref[7], cr, ci)
        xr, xi = xr + pr, xi + pi
        h_ref[rows, :w] = jnp.where(first_row, cr, pltpu.roll(xr, 1, 0))
        h_ref[rows, w:] = jnp.where(first_row, ci, pltpu.roll(xi, 1, 0))
        return xr[ns - 1:ns, :], xi[ns - 1:ns, :]

    cr, ci = lax.fori_loop(0, m // ns, scan, (carry_ref[0:1, :w], carry_ref[0:1, w:]))
    carry_ref[0:1, :w] = cr
    carry_ref[0:1, w:] = ci

    y_state = _dot(h_ref[...].astype(BF16), wc_ref[...])
    for qo in range(nr):
        lo = slice(qo * cb, (qo + 1) * cb)
        acc = y_state[:, lo] + d_ref[:, lo] * u_ref[:, lo].astype(F32)
        for qi in range(qo + 1):
            acc = acc + _dot(u_ref[:, qi * cb:(qi + 1) * cb], t_ref[qo - qi])
        o_ref[:, lo] = jax.nn.gelu(acc).astype(o_ref.dtype)


def _s5_constants(lam_re, lam_im, log_dt, b_re, b_im, c_re, c_im):
    hi = lax.Precision.HIGHEST
    nl, g5, n5 = lam_re.shape
    ch = S5_GROUP_CH
    nr = S5_BLOCK
    gpb = min(S5_GROUPS_PER_BLOCK, g5)
    nblk = g5 // gpb
    dt = jnp.exp(log_dt)[..., None]
    mag = jnp.exp(lam_re * dt)
    ang = lam_im * dt
    abar_re, abar_im = mag * jnp.cos(ang), mag * jnp.sin(ang)
    den = lam_re * lam_re + lam_im * lam_im
    nre, nim = abar_re - 1.0, abar_im
    coef_re = (nre * lam_re + nim * lam_im) / den
    coef_im = (nim * lam_re - nre * lam_im) / den
    bbar_re = coef_re[..., None] * b_re - coef_im[..., None] * b_im
    bbar_im = coef_re[..., None] * b_im + coef_im[..., None] * b_re

    def powers(pr, pi, count):
        out = [(jnp.ones_like(pr), jnp.zeros_like(pi))]
        for _ in range(count):
            out.append(_cmul(out[-1][0], out[-1][1], pr, pi))
        return jnp.stack([o[0] for o in out], axis=1), jnp.stack([o[1] for o in out], axis=1)

    ap_re, ap_im = powers(abar_re, abar_im, nr)
    bp_re, bp_im = powers(ap_re[:, nr], ap_im[:, nr], SUBLANES)

    eye = jnp.eye(gpb, dtype=F32)

    ab_re, ab_im = _cmul(ap_re[:, :nr, :, :, None], ap_im[:, :nr, :, :, None],
                         bbar_re[:, None], bbar_im[:, None])
    taps = (jnp.einsum('lgon,ldgni->ldgoi', c_re, ab_re, precision=hi)
            - jnp.einsum('lgon,ldgni->ldgoi', c_im, ab_im, precision=hi))
    taps = taps.reshape(nl, nr, nblk, gpb, ch, ch).transpose(0, 2, 1, 3, 5, 4)
    ttab = taps[:, :, :, :, :, None, :] * eye[:, None, :, None]
    ttab = ttab.reshape(nl, nblk, nr, gpb * ch, gpb * ch).astype(BF16)

    wb_c = jnp.stack([ab_re[:, ::-1], ab_im[:, ::-1]], axis=2)
    wb_c = wb_c.reshape(nl, nr, 2, nblk, gpb, n5, ch).transpose(0, 3, 1, 4, 6, 2, 5)
    wb = wb_c[:, :, :, :, :, :, None, :] * eye[:, None, None, :, None]
    wb = wb.reshape(nl, nblk, nr * gpb * ch, 2 * gpb * n5).astype(BF16)

    ca_re, ca_im = _cmul(c_re[:, None], c_im[:, None],
                         ap_re[:, 1:, :, None, :], ap_im[:, 1:, :, None, :])
    wc_c = jnp.stack([ca_re, -ca_im], axis=2)
    wc_c = wc_c.reshape(nl, nr, 2, nblk, gpb, ch, n5).transpose(0, 3, 2, 6, 1, 4, 5)
    wc = wc_c[:, :, :, None, :, :, :, :] * eye[:, None, None, :, None]
    wc = wc.reshape(nl, nblk, 2 * gpb * n5, nr * gpb * ch).astype(BF16)

    def lanes(v):
        return v.reshape(nl, v.shape[1], nblk, gpb * n5).transpose(0, 2, 1, 3)

    rows = jnp.arange(SUBLANES)[:, None]
    kinds = []
    for shift in (1, 2, 4):
        for part in (bp_re, bp_im):
            kinds.append(jnp.where(rows >= shift, lanes(part[:, shift:shift + 1]), 0.0))
    for part in (bp_re, bp_im):
        kinds.append(lanes(part[:, 1:]))
    kconst = jnp.stack(kinds, axis=2)
    return ttab, wb, wc, kconst


def _s5_branch(u, consts, layer, d, blocks_per_step=512):
    nblk, s, cb = u.shape
    nr = S5_BLOCK
    ttab, wb, wc, kconst = consts
    w2 = wb.shape[3]
    assert s % nr == 0 and ttab.shape[1:] == (nblk, nr, cb, cb)
    m = _pick_tile(s // nr, blocks_per_step)
    u2 = u.reshape(nblk, s // nr, nr * cb)
    d2 = jnp.tile(d.reshape(nblk, 1, cb), (1, 1, nr))
    out = pl.pallas_call(
        _s5_kernel,
        grid=(nblk, s // nr // m),
        in_specs=[pl.BlockSpec((None, m, nr * cb), lambda gb, t: (gb, t, 0)),
                  pl.BlockSpec((None, None, nr, cb, cb), lambda gb, t: (layer, gb, 0, 0, 0)),
                  pl.BlockSpec((None, None, nr * cb, w2), lambda gb, t: (layer, gb, 0, 0)),
                  pl.BlockSpec((None, None, w2, nr * cb), lambda gb, t: (layer, gb, 0, 0)),
                  pl.BlockSpec((None, None, 8, SUBLANES, w2 // 2), lambda gb, t: (layer, gb, 0, 0, 0)),
                  pl.BlockSpec((None, 1, nr * cb), lambda gb, t: (gb, 0, 0))],
        out_specs=pl.BlockSpec((None, m, nr * cb), lambda gb, t: (gb, t, 0)),
        out_shape=jax.ShapeDtypeStruct((nblk, s // nr, nr * cb), BF16),
        scratch_shapes=[pltpu.VMEM((m, w2), F32), pltpu.VMEM((SUBLANES, w2), F32)],
        compiler_params=_compiler_params(("parallel", "arbitrary")),
        name="s5_branch",
    )(u2, ttab, wb, wc, kconst, d2)
    return out.reshape(nblk, s, cb)


def kernel(x, c, ada_w1, ada_w2, ada_b, ada_table, norm_mix_w, w_in, ssd_conv_w, ssd_conv_b,
           ssd_dt_bias, ssd_a_log, ssd_d, ssd_norm_w, w_ssd_out, s5_lam_re, s5_lam_im, s5_log_dt,
           s5_b_re, s5_b_im, s5_c_re, s5_c_im, s5_d, w_glu, w_out, norm_ffn_w, w_up, ffn_conv_w,
           ffn_conv_b, w_down, final_norm_w):
    bsz, seq, dm = x.shape
    assert bsz == 1
    depth = w_in.shape[0]
    heads = ssd_dt_bias.shape[1]
    inner = heads * SSD_HEAD_DIM
    xbc_w = inner + 2 * SSD_GROUPS * SSD_STATE
    w5 = s5_d.shape[1]
    o_dt = inner + xbc_w
    o_u5 = o_dt + heads
    s5_cb = min(S5_GROUPS_PER_BLOCK, w5 // S5_GROUP_CH) * S5_GROUP_CH

    cpad = jnp.pad(jax.nn.silu(c), ((0, 2 * SUBLANES - bsz), (0, 0))).astype(BF16)
    t1 = _matmul(cpad, ada_w1.astype(BF16), BF16)
    mod = (_matmul(t1, ada_w2.astype(BF16), F32)[0] + ada_b).reshape(N_MOD, dm)

    s5_consts = _s5_constants(s5_lam_re, s5_lam_im, s5_log_dt, s5_b_re, s5_b_im, s5_c_re, s5_c_im)

    xs = x[0]
    for l in range(depth):
        m = mod + ada_table[l]
        h = _norm_mod(xs, norm_mix_w[l] * (1.0 + m[1]), m[0], BF16)
        wl = w_in[l]
        w_main = jnp.concatenate([wl[:, :o_dt], wl[:, o_u5 + w5:]], axis=1).astype(BF16)
        w_dt = jnp.pad(wl[:, o_dt:o_u5], ((0, 0), (0, LANES - heads))).astype(BF16)
        proj = _matmul(h, w_main, BF16)
        u5 = _matmul_slabs(h, wl[:, o_u5:o_u5 + w5].astype(BF16), s5_cb, BF16)
        dt_raw = _matmul(h, w_dt, F32)
        g_ssd = _ssd_branch(proj, dt_raw, ssd_conv_w[l], ssd_conv_b[l], ssd_dt_bias[l],
                            ssd_a_log[l], ssd_d[l], ssd_norm_w[l], inner)
        g_s5 = _s5_branch(u5, s5_consts, l, s5_d[l])
        y_s5 = _matmul_glu(g_s5, w_glu[l].astype(BF16), BF16)
        mix = _matmul_mix(g_ssd, w_ssd_out[l].astype(BF16), proj, o_dt, y_s5, BF16)
        xs = _matmul_resid(mix, w_out[l].astype(BF16), xs, m[2])
        h = _norm_mod(xs, norm_ffn_w[l] * (1.0 + m[4]), m[3], BF16)
        act = _matmul_convgate(h, w_up[l].astype(BF16), ffn_conv_w[l], ffn_conv_b[l], BF16)
        xs = _matmul_resid(act, w_down[l].astype(BF16), xs, m[5])
    out = _norm_mod(xs, final_norm_w, jnp.zeros_like(final_norm_w), F32)
    return out[None]
```

```python
import functools
import math

import jax
import jax.numpy as jnp
from jax import lax
from jax.experimental import pallas as pl
from jax.experimental.pallas import tpu as pltpu

F32 = jnp.float32
BF16 = jnp.bfloat16

SSD_HEAD_DIM = 64
SSD_GROUPS = 8
SSD_STATE = 128
CHUNK = 128
S5_GROUP_CH = 16
N_MOD = 6
EPS = 1e-6

V7X_VMEM_BYTES = 64 * 1024 * 1024
VMEM_LIMIT_BYTES = V7X_VMEM_BYTES - 8 * 1024 * 1024
SUBLANES = 8
LANES = 128
MXU_DIM = 256
S5_GROUPS_PER_BLOCK = MXU_DIM // S5_GROUP_CH
HALO = SUBLANES


def _compiler_params(semantics):
    return pltpu.CompilerParams(dimension_semantics=semantics,
                                vmem_limit_bytes=VMEM_LIMIT_BYTES)


def _pick_tile(dim, want):
    t = min(dim, want)
    while dim % t:
        t //= 2
    return t


def _dot(a, b):
    return jnp.dot(a, b, preferred_element_type=F32)


def _sigmoid(x):
    return jax.nn.sigmoid(x)


def _mm_kernel(a_ref, b_ref, o_ref, *scratch, nk):
    if nk == 1:
        o_ref[...] = _dot(a_ref[...], b_ref[...]).astype(o_ref.dtype)
        return
    acc_ref, = scratch
    k = pl.program_id(2)

    @pl.when(k == 0)
    def _():
        acc_ref[...] = jnp.zeros_like(acc_ref)

    acc_ref[...] += _dot(a_ref[...], b_ref[...])

    @pl.when(k == nk - 1)
    def _():
        o_ref[...] = acc_ref[...].astype(o_ref.dtype)


def _k_tiling(k, tk):
    if k % tk:
        tk = k // 2 if k > tk else k
    tk = min(tk, k)
    assert k % tk == 0
    return tk, k // tk


def _matmul(a, b, out_dtype, tm=1024, tn=1024, tk=4096):
    m, k = a.shape
    n = b.shape[1]
    tm, tn = _pick_tile(m, tm), _pick_tile(n, tn)
    tk, nk = _k_tiling(k, tk)
    scratch = [pltpu.VMEM((tm, tn), F32)] if nk > 1 else []
    return pl.pallas_call(
        functools.partial(_mm_kernel, nk=nk),
        grid=(m // tm, n // tn, nk),
        in_specs=[pl.BlockSpec((tm, tk), lambda i, j, kk: (i, kk)),
                  pl.BlockSpec((tk, tn), lambda i, j, kk: (kk, j))],
        out_specs=pl.BlockSpec((tm, tn), lambda i, j, kk: (i, j)),
        out_shape=jax.ShapeDtypeStruct((m, n), out_dtype),
        scratch_shapes=scratch,
        compiler_params=_compiler_params(("parallel", "parallel", "arbitrary")),
        name="matmul",
    )(a, b)


def _col_slices(tn):
    cs = min(tn, MXU_DIM)
    return [slice(c, c + cs) for c in range(0, tn, cs)]


def _mm_slab_kernel(a_ref, b_ref, o_ref):
    acc = _dot(a_ref[...], b_ref[...])
    cb = o_ref.shape[2]
    for sidx in range(o_ref.shape[0]):
        o_ref[sidx] = acc[:, sidx * cb:(sidx + 1) * cb].astype(o_ref.dtype)


def _matmul_slabs(a, w, cb, out_dtype, tm=1024, tn=1024):
    m, k = a.shape
    n = w.shape[1]
    tm, tn = _pick_tile(m, tm), _pick_tile(n, tn)
    assert tn % cb == 0
    spb = tn // cb
    return pl.pallas_call(
        _mm_slab_kernel,
        grid=(m // tm, n // tn),
        in_specs=[pl.BlockSpec((tm, k), lambda i, j: (i, 0)),
                  pl.BlockSpec((k, tn), lambda i, j: (0, j))],
        out_specs=pl.BlockSpec((spb, tm, cb), lambda i, j: (j, i, 0)),
        out_shape=jax.ShapeDtypeStruct((n // cb, m, cb), out_dtype),
        compiler_params=_compiler_params(("parallel", "parallel")),
        name="matmul_slabs",
    )(a, w)


def _mm_glu_kernel(*refs, nslab):
    a_refs, (bv_ref, bg_ref, o_ref) = refs[:nslab], refs[nslab:]
    a = jnp.concatenate([r[...] for r in a_refs], axis=1)
    for sl in _col_slices(o_ref.shape[1]):
        v = _dot(a, bv_ref[:, sl])
        g = _dot(a, bg_ref[:, sl])
        o_ref[:, sl] = (v * _sigmoid(g)).astype(o_ref.dtype)


def _matmul_glu(a_slabs, w, out_dtype, tm=1024, tn=1024):
    nslab, m, cb = a_slabs.shape
    k = nslab * cb
    n = w.shape[1] // 2
    tm, tn = _pick_tile(m, tm), _pick_tile(n, tn)
    nj = n // tn
    slab_specs = [pl.BlockSpec((None, tm, cb), functools.partial(lambda i, j, sidx: (sidx, i, 0), sidx=sidx))
                  for sidx in range(nslab)]
    return pl.pallas_call(
        functools.partial(_mm_glu_kernel, nslab=nslab),
        grid=(m // tm, nj),
        in_specs=slab_specs + [pl.BlockSpec((k, tn), lambda i, j: (0, j)),
                               pl.BlockSpec((k, tn), lambda i, j: (0, j + nj))],
        out_specs=pl.BlockSpec((tm, tn), lambda i, j: (i, j)),
        out_shape=jax.ShapeDtypeStruct((m, n), out_dtype),
        compiler_params=_compiler_params(("parallel", "parallel")),
        name="matmul_glu",
    )(*([a_slabs] * nslab), w, w)


def _mm_mix_kernel(a_ref, b_ref, g1_ref, g2_ref, y2_ref, o_ref):
    a = a_ref[...]
    for sl in _col_slices(o_ref.shape[1]):
        y1 = _dot(a, b_ref[:, sl])
        o_ref[:, sl] = (_sigmoid(g1_ref[:, sl].astype(F32)) * y1
                        + _sigmoid(g2_ref[:, sl].astype(F32)) * y2_ref[:, sl].astype(F32)
                        ).astype(o_ref.dtype)


def _matmul_mix(a, w, proj, gate_col, y2, out_dtype, tm=1024):
    m, k = a.shape
    n = w.shape[1]
    tm = _pick_tile(m, tm)
    tn = math.gcd(math.gcd(512, gate_col), n)
    nj = n // tn
    g0 = gate_col // tn
    return pl.pallas_call(
        _mm_mix_kernel,
        grid=(m // tm, nj),
        in_specs=[pl.BlockSpec((tm, k), lambda i, j: (i, 0)),
                  pl.BlockSpec((k, tn), lambda i, j: (0, j)),
                  pl.BlockSpec((tm, tn), lambda i, j: (i, g0 + j)),
                  pl.BlockSpec((tm, tn), lambda i, j: (i, g0 + nj + j)),
                  pl.BlockSpec((tm, tn), lambda i, j: (i, j))],
        out_specs=pl.BlockSpec((tm, tn), lambda i, j: (i, j)),
        out_shape=jax.ShapeDtypeStruct((m, n), out_dtype),
        compiler_params=_compiler_params(("parallel", "parallel")),
        name="matmul_mix",
    )(a, w, proj, proj, y2)


def _mm_resid_kernel(a_ref, b_ref, x_ref, g_ref, o_ref, *scratch, nk):
    if nk == 1:
        a = a_ref[...]
        for sl in _col_slices(o_ref.shape[1]):
            o_ref[:, sl] = x_ref[:, sl] + g_ref[:, sl] * _dot(a, b_ref[:, sl])
        return
    acc_ref, = scratch
    k = pl.program_id(2)

    @pl.when(k == 0)
    def _():
        acc_ref[...] = jnp.zeros_like(acc_ref)

    acc_ref[...] += _dot(a_ref[...], b_ref[...])

    @pl.when(k == nk - 1)
    def _():
        o_ref[...] = x_ref[...] + g_ref[...] * acc_ref[...]


def _matmul_resid(a, w, x, gate, tm=1024, tk=4096):
    m, k = a.shape
    n = w.shape[1]
    tk, nk = _k_tiling(k, tk)
    tm, tn = _pick_tile(m, tm), _pick_tile(n, 1024 if nk > 1 else 512)
    scratch = [pltpu.VMEM((tm, tn), F32)] if nk > 1 else []
    return pl.pallas_call(
        functools.partial(_mm_resid_kernel, nk=nk),
        grid=(m // tm, n // tn, nk),
        in_specs=[pl.BlockSpec((tm, tk), lambda i, j, kk: (i, kk)),
                  pl.BlockSpec((tk, tn), lambda i, j, kk: (kk, j)),
                  pl.BlockSpec((tm, tn), lambda i, j, kk: (i, j)),
                  pl.BlockSpec((1, tn), lambda i, j, kk: (0, j))],
        out_specs=pl.BlockSpec((tm, tn), lambda i, j, kk: (i, j)),
        out_shape=jax.ShapeDtypeStruct((m, n), F32),
        scratch_shapes=scratch,
        compiler_params=_compiler_params(("parallel", "parallel", "arbitrary")),
        name="matmul_resid",
    )(a, w, x, gate.reshape(1, n))


def _conv_taps(buf_ref, w_ref, b_ref, base, rows, sl):
    k = w_ref.shape[0]
    out = b_ref[:, sl] + w_ref[k - 1:k, sl] * buf_ref[pl.ds(base, rows), sl]
    for j in range(1, k):
        out = out + w_ref[k - 1 - j:k - j, sl] * buf_ref[pl.ds(base - j, rows), sl]
    return out


def _mm_convgate_kernel(a_ref, ba_ref, bv_ref, wa_ref, wv_ref, ca_ref, cv_ref, o_ref,
                        ua_ref, uv_ref, halo_ref):
    i = pl.program_id(0)
    j = pl.program_id(1)
    tm = a_ref.shape[0]

    @pl.when(i == 0)
    def _():
        halo_ref[j] = jnp.zeros(halo_ref.shape[1:], F32)

    ua_ref[0:HALO, :] = halo_ref[j, 0]
    uv_ref[0:HALO, :] = halo_ref[j, 1]
    a = a_ref[...]
    for sl in _col_slices(o_ref.shape[1]):
        ua_ref[HALO:, sl] = _dot(a, ba_ref[:, sl])
        uv_ref[HALO:, sl] = _dot(a, bv_ref[:, sl])
        ca = _conv_taps(ua_ref, wa_ref, ca_ref, HALO, tm, sl)
        cv = _conv_taps(uv_ref, wv_ref, cv_ref, HALO, tm, sl)
        o_ref[:, sl] = (ca * _sigmoid(ca) * cv).astype(o_ref.dtype)
    halo_ref[j, 0] = ua_ref[tm:tm + HALO, :]
    halo_ref[j, 1] = uv_ref[tm:tm + HALO, :]


def _matmul_convgate(a, w, conv_w, conv_b, out_dtype, tm=1024):
    m, k = a.shape
    n = w.shape[1] // 2
    kc = conv_w.shape[0]
    assert kc - 1 <= HALO
    tm = _pick_tile(m, tm)
    tn = math.gcd(512, n)
    nj = n // tn
    conv_b = conv_b.reshape(1, 2 * n)
    return pl.pallas_call(
        _mm_convgate_kernel,
        grid=(m // tm, nj),
        in_specs=[pl.BlockSpec((tm, k), lambda i, j: (i, 0)),
                  pl.BlockSpec((k, tn), lambda i, j: (0, j)),
                  pl.BlockSpec((k, tn), lambda i, j: (0, j + nj)),
                  pl.BlockSpec((kc, tn), lambda i, j: (0, j)),
                  pl.BlockSpec((kc, tn), lambda i, j: (0, j + nj)),
                  pl.BlockSpec((1, tn), lambda i, j: (0, j)),
                  pl.BlockSpec((1, tn), lambda i, j: (0, j + nj))],
        out_specs=pl.BlockSpec((tm, tn), lambda i, j: (i, j)),
        out_shape=jax.ShapeDtypeStruct((m, n), out_dtype),
        scratch_shapes=[pltpu.VMEM((tm + HALO, tn), F32), pltpu.VMEM((tm + HALO, tn), F32),
                        pltpu.VMEM((nj, 2, HALO, tn), F32)],
        compiler_params=_compiler_params(("arbitrary", "arbitrary")),
        name="matmul_convgate",
    )(a, w, w, conv_w, conv_w, conv_b, conv_b)


def _norm_kernel(x_ref, w_ref, s_ref, o_ref):
    x = x_ref[...]
    ms = jnp.mean(x * x, axis=-1, keepdims=True)
    o_ref[...] = (x * lax.rsqrt(ms + EPS) * w_ref[...] + s_ref[...]).astype(o_ref.dtype)


def _norm_mod(x, w, shift, out_dtype, tr=256):
    s, d = x.shape
    tr = _pick_tile(s, tr)
    return pl.pallas_call(
        _norm_kernel,
        grid=(s // tr,),
        in_specs=[pl.BlockSpec((tr, d), lambda i: (i, 0)),
                  pl.BlockSpec((1, d), lambda i: (0, 0)),
                  pl.BlockSpec((1, d), lambda i: (0, 0))],
        out_specs=pl.BlockSpec((tr, d), lambda i: (i, 0)),
        out_shape=jax.ShapeDtypeStruct((s, d), out_dtype),
        compiler_params=_compiler_params(("parallel",)),
        name="norm_mod",
    )(x, w.reshape(1, d), shift.reshape(1, d))


def _split3(x):
    hi = x.astype(BF16)
    r1 = x - hi.astype(F32)
    mid = r1.astype(BF16)
    lo = (r1 - mid.astype(F32)).astype(BF16)
    return hi, mid, lo


def _dot01_left(t01, x):
    hi, mid, lo = _split3(x)
    return _dot(t01, hi) + _dot(t01, mid) + _dot(t01, lo)


def _dot01_right(x, t01):
    hi, mid, lo = _split3(x)
    return _dot(hi, t01) + _dot(mid, t01) + _dot(lo, t01)


def _softplus(x):
    return jnp.maximum(x, 0.0) + jnp.log1p(jnp.exp(-jnp.abs(x)))


def _ssd_kernel(x_ref, b_ref, c_ref, z_ref, dt_ref, dtT_ref, bias_ref, biasT_ref,
                wx_ref, wb_ref, wc_ref, cbx_ref, cbb_ref, cbc_ref,
                arow_ref, acol_ref, e_ref, dexp_ref, nw_ref,
                o_ref, state_ref, xbuf_ref, bbuf_ref, cbuf_ref, *, nchunk, hpg):
    q = CHUNK
    p = SSD_HEAD_DIM
    r = x_ref.shape[0]

    @pl.when(pl.program_id(1) == 0)
    def _():
        state_ref[...] = jnp.zeros_like(state_ref)
        xbuf_ref[0:HALO, :] = jnp.zeros((HALO, xbuf_ref.shape[1]), F32)
        bbuf_ref[0:HALO, :] = jnp.zeros((HALO, bbuf_ref.shape[1]), F32)
        cbuf_ref[0:HALO, :] = jnp.zeros((HALO, cbuf_ref.shape[1]), F32)

    xbuf_ref[HALO:, :] = x_ref[...].astype(F32)
    bbuf_ref[HALO:, :] = b_ref[...].astype(F32)
    cbuf_ref[HALO:, :] = c_ref[...].astype(F32)

    row = lax.broadcasted_iota(jnp.int32, (q, q), 0)
    col = lax.broadcasted_iota(jnp.int32, (q, q), 1)
    mask = row >= col
    tri = mask.astype(BF16)
    tri_t = (row <= col).astype(BF16)
    e = e_ref[...]
    dexp = dexp_ref[...]
    acol = acol_ref[...]
    nw = nw_ref[...]
    full = slice(None)

    def conv_silu(buf_ref, w_ref, cb_ref, base):
        v = _conv_taps(buf_ref, w_ref, cb_ref, base, q, full)
        return v * _sigmoid(v)

    for ci in range(nchunk):
        sl = pl.ds(ci * q, q)
        base = HALO + ci * q
        x = conv_silu(xbuf_ref, wx_ref, cbx_ref, base)
        b = conv_silu(bbuf_ref, wb_ref, cbb_ref, base).astype(BF16)
        c = conv_silu(cbuf_ref, wc_ref, cbc_ref, base).astype(BF16)
        dt = _softplus(dt_ref[sl, :] + bias_ref[...])
        dt_t = _softplus(dtT_ref[:, sl] + biasT_ref[...])
        dt_exp = _dot01_right(dt, e)
        cum_h = _dot01_left(tri, dt * arow_ref[...])
        cum = _dot01_right(cum_h, e)
        cum_t = _dot01_right(dt_t * acol, tri_t)
        cum_last = cum[q - 1:q, :]
        xdt = x * dt_exp
        cb = lax.dot_general(c, b, (((1,), (1,)), ((), ())), preferred_element_type=F32)
        st = state_ref[...]
        y_inter = _dot(c, st.astype(BF16)) * jnp.exp(cum)
        ys = []
        for h in range(hpg):
            seg = cum[:, h * p:h * p + 1] - cum_t[h:h + 1, :]
            dec = jnp.exp(jnp.where(mask, seg, -jnp.inf))
            w = (cb * dec).astype(BF16)
            ys.append(_dot(w, xdt[:, h * p:(h + 1) * p].astype(BF16)))
        y = jnp.concatenate(ys, axis=1) + y_inter + dexp * x
        dec_end = jnp.exp(cum_last - cum)
        b_t = b.astype(F32).T.astype(BF16)
        state_ref[...] = st * jnp.exp(cum_last) + _dot(b_t, (xdt * dec_end).astype(BF16))
        zf = z_ref[sl, :].astype(F32)
        gt = y * (zf * _sigmoid(zf))
        ms = jnp.mean(gt * gt, axis=-1, keepdims=True)
        o_ref[sl, :] = (gt * lax.rsqrt(ms + EPS) * nw).astype(o_ref.dtype)

    xbuf_ref[0:HALO, :] = xbuf_ref[r:r + HALO, :]
    bbuf_ref[0:HALO, :] = bbuf_ref[r:r + HALO, :]
    cbuf_ref[0:HALO, :] = cbuf_ref[r:r + HALO, :]


def _ssd_branch(proj, dt_raw, conv_w, conv_b, dt_bias, a_log, d, norm_w, inner, rows_per_step=1024):
    s = proj.shape[0]
    h = dt_bias.shape[0]
    g = SSD_GROUPS
    hpg = h // g
    gw = hpg * SSD_HEAD_DIM
    n = SSD_STATE
    r = _pick_tile(s, rows_per_step)
    kc = conv_w.shape[0]
    assert r % CHUNK == 0 and inner == g * gw and gw % LANES == 0 and h <= LANES and kc - 1 <= HALO
    a = -jnp.exp(a_log)
    dt_t = dt_raw[:, :h].T.reshape(g, hpg, s)
    bias = jnp.pad(dt_bias, (0, LANES - h)).reshape(1, LANES)
    bias_t = jnp.broadcast_to(dt_bias.reshape(g, hpg, 1), (g, hpg, CHUNK))
    arow = jnp.pad(a, (0, LANES - h)).reshape(1, LANES)
    dexp = jnp.repeat(d, SSD_HEAD_DIM).reshape(g, 1, gw)
    acol = jnp.broadcast_to(a.reshape(g, hpg, 1), (g, hpg, CHUNK))
    lane_head = jnp.arange(inner) // SSD_HEAD_DIM
    e = (jnp.arange(LANES)[:, None] == lane_head[None, :]).astype(BF16)
    e = e.reshape(LANES, g, gw).transpose(1, 0, 2)
    conv_b = conv_b.reshape(1, -1)
    xb0 = inner // gw
    bb0 = 2 * inner // n
    cb0 = bb0 + g
    wb0 = inner // n
    wc0 = wb0 + g
    per_g = lambda gi, t: (gi, 0, 0)
    return pl.pallas_call(
        functools.partial(_ssd_kernel, nchunk=r // CHUNK, hpg=hpg),
        grid=(g, s // r),
        in_specs=[pl.BlockSpec((r, gw), lambda gi, t: (t, xb0 + gi)),
                  pl.BlockSpec((r, n), lambda gi, t: (t, bb0 + gi)),
                  pl.BlockSpec((r, n), lambda gi, t: (t, cb0 + gi)),
                  pl.BlockSpec((r, gw), lambda gi, t: (t, gi)),
                  pl.BlockSpec((r, LANES), lambda gi, t: (t, 0)),
                  pl.BlockSpec((None, hpg, r), lambda gi, t: (gi, 0, t)),
                  pl.BlockSpec((1, LANES), lambda gi, t: (0, 0)),
                  pl.BlockSpec((None, hpg, CHUNK), per_g),
                  pl.BlockSpec((kc, gw), lambda gi, t: (0, gi)),
                  pl.BlockSpec((kc, n), lambda gi, t: (0, wb0 + gi)),
                  pl.BlockSpec((kc, n), lambda gi, t: (0, wc0 + gi)),
                  pl.BlockSpec((1, gw), lambda gi, t: (0, gi)),
                  pl.BlockSpec((1, n), lambda gi, t: (0, wb0 + gi)),
                  pl.BlockSpec((1, n), lambda gi, t: (0, wc0 + gi)),
                  pl.BlockSpec((1, LANES), lambda gi, t: (0, 0)),
                  pl.BlockSpec((None, hpg, CHUNK), per_g),
                  pl.BlockSpec((None, LANES, gw), per_g),
                  pl.BlockSpec((None, 1, gw), per_g),
                  pl.BlockSpec((1, gw), lambda gi, t: (0, gi))],
        out_specs=pl.BlockSpec((r, gw), lambda gi, t: (t, gi)),
        out_shape=jax.ShapeDtypeStruct((s, inner), BF16),
        scratch_shapes=[pltpu.VMEM((n, gw), F32), pltpu.VMEM((HALO + r, gw), F32),
                        pltpu.VMEM((HALO + r, n), F32), pltpu.VMEM((HALO + r, n), F32)],
        compiler_params=_compiler_params(("parallel", "arbitrary")),
        name="ssd_branch",
    )(proj, proj, proj, proj, dt_raw, dt_t, bias, bias_t, conv_w, conv_w, conv_w,
      conv_b, conv_b, conv_b, arow, acol, e, dexp, norm_w.reshape(1, inner))


S5_BLOCK = 8


def _cmul(ar, ai, br, bi):
    return ar * br - ai * bi, ar * bi + ai * br


def _log2(n):
    assert n > 0 and n & (n - 1) == 0
    return n.bit_length() - 1


def _s5_expand_weights(ab_ref, ac_ref, kc_ref, wb_ref, wc_ref, t_ref):
    rcb, cn = ab_ref.shape
    w2 = wb_ref.shape[1]
    cb = t_ref.shape[1]
    n5 = cn // 2
    gpb = w2 // cn
    ch = cb // gpb
    step = MXU_DIM
    iota = lambda shape, dim: lax.broadcasted_iota(jnp.int32, shape, dim)

    def group(idx, unit):
        return (idx >> _log2(unit)) & (gpb - 1)

    for c0 in range(0, w2, step):
        r = iota((cn, step), 0)
        j = iota((cn, step), 1) + c0
        onehot = ((r >> _log2(n5)) == (j >> _log2(gpb * n5))) & ((r & (n5 - 1)) == (j & (n5 - 1)))
        spread = _dot(ab_ref[...], onehot.astype(BF16))
        keep = group(iota((rcb, step), 0), ch) == group(iota((rcb, step), 1) + c0, n5)
        wb_ref[:, c0:c0 + step] = jnp.where(keep, spread, 0.0).astype(BF16)
    for r0 in range(0, w2, step):
        i = iota((step, cn), 0) + r0
        r = iota((step, cn), 1)
        onehot = ((r >> _log2(n5)) == (i >> _log2(gpb * n5))) & ((r & (n5 - 1)) == (i & (n5 - 1)))
        spread = _dot(onehot.astype(BF16), ac_ref[...])
        keep = group(iota((step, rcb), 0) + r0, n5) == group(iota((step, rcb), 1), ch)
        wc_ref[r0:r0 + step, :] = jnp.where(keep, spread, 0.0).astype(BF16)
    r = iota((kc_ref.shape[1], cb), 0)
    j = iota((kc_ref.shape[1], cb), 1)
    spread = _dot(kc_ref[...], (r == (j & (ch - 1))).astype(BF16))
    keep = group(iota((rcb, cb), 0), ch) == group(iota((rcb, cb), 1), ch)
    t_ref[...] = jnp.where(keep, spread, 0.0).astype(BF16)


def _s5_kernel(u_ref, ab_ref, ac_ref, kc_ref, k_ref, d_ref, o_ref,
               wb_ref, wc_ref, t_ref, h_ref, carry_ref):
    m, rcb = u_ref.shape
    cb = t_ref.shape[1]
    nr = rcb // cb
    w = h_ref.shape[1] // 2
    ns = SUBLANES

    @pl.when(pl.program_id(1) == 0)
    def _():
        carry_ref[...] = jnp.zeros_like(carry_ref)
        _s5_expand_weights(ab_ref, ac_ref, kc_ref, wb_ref, wc_ref, t_ref)

    h_ref[...] = _dot(u_ref[...], wb_ref[...])
    first_row = lax.broadcasted_iota(jnp.int32, (ns, w), 0) == 0

    def scan(r, carry):
        cr, ci = carry
        rows = pl.ds(pl.multiple_of(r * ns, ns), ns)
        xr = h_ref[rows, :w]
        xi = h_ref[rows, w:]
        for idx, shift in enumerate((1, 2, 4)):
            pr, pi = _cmul(k_ref[2 * idx], k_ref[2 * idx + 1],
                           pltpu.roll(xr, shift, 0), pltpu.roll(xi, shift, 0))
            xr, xi = xr + pr, xi + pi
        pr, pi = _cmul(k_ref[6], k_ref[7], cr, ci)
        xr, xi = xr + pr, xi + pi
        h_ref[rows, :w] = jnp.where(first_row, cr, pltpu.roll(xr, 1, 0))
        h_ref[rows, w:] = jnp.where(first_row, ci, pltpu.roll(xi, 1, 0))
        return xr[ns - 1:ns, :], xi[ns - 1:ns, :]

    cr, ci = lax.fori_loop(0, m // ns, scan, (carry_ref[0:1, :w], carry_ref[0:1, w:]))
    carry_ref[0:1, :w] = cr
    carry_ref[0:1, w:] = ci

    y_state = _dot(h_ref[...].astype(BF16), wc_ref[...])
    for qo in range(nr):
        lo = slice(qo * cb, (qo + 1) * cb)
        acc = y_state[:, lo] + d_ref[:, lo] * u_ref[:, lo].astype(F32)
        for qi in range(qo + 1):
            acc = acc + _dot(u_ref[:, qi * cb:(qi + 1) * cb], t_ref[(qo - qi) * cb:(qo - qi + 1) * cb, :])
        o_ref[:, lo] = jax.nn.gelu(acc).astype(o_ref.dtype)


def _s5_constants(lam_re, lam_im, log_dt, b_re, b_im, c_re, c_im):
    hi = lax.Precision.HIGHEST
    nl, g5, n5 = lam_re.shape
    ch = S5_GROUP_CH
    nr = S5_BLOCK
    gpb = min(S5_GROUPS_PER_BLOCK, g5)
    nblk = g5 // gpb
    dt = jnp.exp(log_dt)[..., None]
    mag = jnp.exp(lam_re * dt)
    ang = lam_im * dt
    abar_re, abar_im = mag * jnp.cos(ang), mag * jnp.sin(ang)
    den = lam_re * lam_re + lam_im * lam_im
    nre, nim = abar_re - 1.0, abar_im
    coef_re = (nre * lam_re + nim * lam_im) / den
    coef_im = (nim * lam_re - nre * lam_im) / den
    bbar_re = coef_re[..., None] * b_re - coef_im[..., None] * b_im
    bbar_im = coef_re[..., None] * b_im + coef_im[..., None] * b_re

    def powers(pr, pi, count):
        out = [(jnp.ones_like(pr), jnp.zeros_like(pi))]
        for _ in range(count):
            out.append(_cmul(out[-1][0], out[-1][1], pr, pi))
        return jnp.stack([o[0] for o in out], axis=1), jnp.stack([o[1] for o in out], axis=1)

    ap_re, ap_im = powers(abar_re, abar_im, nr)
    bp_re, bp_im = powers(ap_re[:, nr], ap_im[:, nr], SUBLANES)

    ab_re, ab_im = _cmul(ap_re[:, :nr, :, :, None], ap_im[:, :nr, :, :, None],
                         bbar_re[:, None], bbar_im[:, None])
    taps = (jnp.einsum('lgon,ldgni->ldgoi', c_re, ab_re, precision=hi)
            - jnp.einsum('lgon,ldgni->ldgoi', c_im, ab_im, precision=hi))
    kc = taps.reshape(nl, nr, nblk, gpb, ch, ch).transpose(0, 2, 1, 3, 5, 4)
    kc = jnp.pad(kc.reshape(nl, nblk, nr * gpb * ch, ch), ((0, 0), (0, 0), (0, 0), (0, LANES - ch)))

    ab = jnp.stack([ab_re[:, ::-1], ab_im[:, ::-1]], axis=2)
    ab = ab.reshape(nl, nr, 2, nblk, gpb, n5, ch).transpose(0, 3, 1, 4, 6, 2, 5)
    ab = ab.reshape(nl, nblk, nr * gpb * ch, 2 * n5)

    ca_re, ca_im = _cmul(c_re[:, None], c_im[:, None],
                         ap_re[:, 1:, :, None, :], ap_im[:, 1:, :, None, :])
    ac = jnp.stack([ca_re, -ca_im], axis=2)
    ac = ac.reshape(nl, nr, 2, nblk, gpb, ch, n5).transpose(0, 3, 2, 6, 1, 4, 5)
    ac = ac.reshape(nl, nblk, 2 * n5, nr * gpb * ch)

    def lanes(v):
        return v.reshape(nl, v.shape[1], nblk, gpb * n5).transpose(0, 2, 1, 3)

    rows = jnp.arange(SUBLANES)[:, None]
    kinds = []
    for shift in (1, 2, 4):
        for part in (bp_re, bp_im):
            kinds.append(jnp.where(rows >= shift, lanes(part[:, shift:shift + 1]), 0.0))
    for part in (bp_re, bp_im):
        kinds.append(lanes(part[:, 1:]))
    kconst = jnp.stack(kinds, axis=2)
    return ab.astype(BF16), ac.astype(BF16), kc.astype(BF16), kconst


def _s5_branch(u, consts, layer, d, blocks_per_step=512):
    nblk, s, cb = u.shape
    nr = S5_BLOCK
    ab, ac, kc, kconst = consts
    w2 = 2 * kconst.shape[4]
    assert s % nr == 0 and ab.shape[1:3] == (nblk, nr * cb)
    m = _pick_tile(s // nr, blocks_per_step)
    u2 = u.reshape(nblk, s // nr, nr * cb)
    d2 = jnp.tile(d.reshape(nblk, 1, cb), (1, 1, nr))
    out = pl.pallas_call(
        _s5_kernel,
        grid=(nblk, s // nr // m),
        in_specs=[pl.BlockSpec((None, m, nr * cb), lambda gb, t: (gb, t, 0)),
                  pl.BlockSpec((None, None) + ab.shape[2:], lambda gb, t: (layer, gb, 0, 0)),
                  pl.BlockSpec((None, None) + ac.shape[2:], lambda gb, t: (layer, gb, 0, 0)),
                  pl.BlockSpec((None, None) + kc.shape[2:], lambda gb, t: (layer, gb, 0, 0)),
                  pl.BlockSpec((None, None, 8, SUBLANES, w2 // 2), lambda gb, t: (layer, gb, 0, 0, 0)),
                  pl.BlockSpec((None, 1, nr * cb), lambda gb, t: (gb, 0, 0))],
        out_specs=pl.BlockSpec((None, m, nr * cb), lambda gb, t: (gb, t, 0)),
        out_shape=jax.ShapeDtypeStruct((nblk, s // nr, nr * cb), BF16),
        scratch_shapes=[pltpu.VMEM((nr * cb, w2), BF16), pltpu.VMEM((w2, nr * cb), BF16),
                        pltpu.VMEM((nr * cb, cb), BF16),
                        pltpu.VMEM((m, w2), F32), pltpu.VMEM((SUBLANES, w2), F32)],
        compiler_params=_compiler_params(("parallel", "arbitrary")),
        name="s5_branch",
    )(u2, ab, ac, kc, kconst, d2)
    return out.reshape(nblk, s, cb)


def kernel(x, c, ada_w1, ada_w2, ada_b, ada_table, norm_mix_w, w_in, ssd_conv_w, ssd_conv_b,
           ssd_dt_bias, ssd_a_log, ssd_d, ssd_norm_w, w_ssd_out, s5_lam_re, s5_lam_im, s5_log_dt,
           s5_b_re, s5_b_im, s5_c_re, s5_c_im, s5_d, w_glu, w_out, norm_ffn_w, w_up, ffn_conv_w,
           ffn_conv_b, w_down, final_norm_w):
    bsz, seq, dm = x.shape
    assert bsz == 1
    depth = w_in.shape[0]
    heads = ssd_dt_bias.shape[1]
    inner = heads * SSD_HEAD_DIM
    xbc_w = inner + 2 * SSD_GROUPS * SSD_STATE
    w5 = s5_d.shape[1]
    o_dt = inner + xbc_w
    o_u5 = o_dt + heads
    s5_cb = min(S5_GROUPS_PER_BLOCK, w5 // S5_GROUP_CH) * S5_GROUP_CH

    cpad = jnp.pad(jax.nn.silu(c), ((0, 2 * SUBLANES - bsz), (0, 0))).astype(BF16)
    t1 = _matmul(cpad, ada_w1.astype(BF16), BF16)
    mod = (_matmul(t1, ada_w2.astype(BF16), F32)[0] + ada_b).reshape(N_MOD, dm)

    s5_consts = _s5_constants(s5_lam_re, s5_lam_im, s5_log_dt, s5_b_re, s5_b_im, s5_c_re, s5_c_im)

    xs = x[0]
    for l in range(depth):
        m = mod + ada_table[l]
        h = _norm_mod(xs, norm_mix_w[l] * (1.0 + m[1]), m[0], BF16)
        wl = w_in[l]
        w_main = jnp.concatenate([wl[:, :o_dt], wl[:, o_u5 + w5:]], axis=1).astype(BF16)
        w_dt = jnp.pad(wl[:, o_dt:o_u5], ((0, 0), (0, LANES - heads))).astype(BF16)
        proj = _matmul(h, w_main, BF16)
        u5 = _matmul_slabs(h, wl[:, o_u5:o_u5 + w5].astype(BF16), s5_cb, BF16)
        dt_raw = _matmul(h, w_dt, F32)
        g_ssd = _ssd_branch(proj, dt_raw, ssd_conv_w[l], ssd_conv_b[l], ssd_dt_bias[l],
                            ssd_a_log[l], ssd_d[l], ssd_norm_w[l], inner)
        g_s5 = _s5_branch(u5, s5_consts, l, s5_d[l])
        y_s5 = _matmul_glu(g_s5, w_glu[l].astype(BF16), BF16)
        mix = _matmul_mix(g_ssd, w_ssd_out[l].astype(BF16), proj, o_dt, y_s5, BF16)
        xs = _matmul_resid(mix, w_out[l].astype(BF16), xs, m[2])
        h = _norm_mod(xs, norm_ffn_w[l] * (1.0 + m[4]), m[3], BF16)
        act = _matmul_convgate(h, w_up[l].astype(BF16), ffn_conv_w[l], ffn_conv_b[l], BF16)
        xs = _matmul_resid(act, w_down[l].astype(BF16), xs, m[5])
    out = _norm_mod(xs, final_norm_w, jnp.zeros_like(final_norm_w), F32)
    return out[None]
```

```python
import functools
import math

import jax
import jax.numpy as jnp
from jax import lax
from jax.experimental import pallas as pl
from jax.experimental.pallas import tpu as pltpu

F32 = jnp.float32
BF16 = jnp.bfloat16

SSD_HEAD_DIM = 64
SSD_GROUPS = 8
SSD_STATE = 128
CHUNK = 128
S5_GROUP_CH = 16
N_MOD = 6
EPS = 1e-6

V7X_VMEM_BYTES = 64 * 1024 * 1024
VMEM_LIMIT_BYTES = V7X_VMEM_BYTES - 8 * 1024 * 1024
SUBLANES = 8
LANES = 128
MXU_DIM = 256
S5_GROUPS_PER_BLOCK = MXU_DIM // S5_GROUP_CH
HALO = SUBLANES


def _compiler_params(semantics):
    return pltpu.CompilerParams(dimension_semantics=semantics,
                                vmem_limit_bytes=VMEM_LIMIT_BYTES)


def _pick_tile(dim, want):
    t = min(dim, want)
    while dim % t:
        t //= 2
    return t


def _dot(a, b):
    return jnp.dot(a, b, preferred_element_type=F32)


def _weight_spec(w, layer, block, index_map):
    if layer is None:
        assert w.ndim == 2
        return pl.BlockSpec(block, index_map)
    assert w.ndim == 3
    return pl.BlockSpec((None,) + block, lambda *idx: (layer,) + index_map(*idx))


def _sigmoid(x):
    return jax.nn.sigmoid(x)


def _mm_kernel(a_ref, b_ref, o_ref, *scratch, nk):
    if nk == 1:
        o_ref[...] = _dot(a_ref[...], b_ref[...]).astype(o_ref.dtype)
        return
    acc_ref, = scratch
    k = pl.program_id(2)

    @pl.when(k == 0)
    def _():
        acc_ref[...] = jnp.zeros_like(acc_ref)

    acc_ref[...] += _dot(a_ref[...], b_ref[...])

    @pl.when(k == nk - 1)
    def _():
        o_ref[...] = acc_ref[...].astype(o_ref.dtype)


def _k_tiling(k, tk):
    if k % tk:
        tk = k // 2 if k > tk else k
    tk = min(tk, k)
    assert k % tk == 0
    return tk, k // tk


def _matmul(a, b, out_dtype, tm=1024, tn=1024, tk=4096, layer=None, n=None):
    m, k = a.shape
    n = b.shape[-1] if n is None else n
    tm, tn = _pick_tile(m, tm), _pick_tile(n, tn)
    tk, nk = _k_tiling(k, tk)
    scratch = [pltpu.VMEM((tm, tn), F32)] if nk > 1 else []
    return pl.pallas_call(
        functools.partial(_mm_kernel, nk=nk),
        grid=(m // tm, n // tn, nk),
        in_specs=[pl.BlockSpec((tm, tk), lambda i, j, kk: (i, kk)),
                  _weight_spec(b, layer, (tk, tn), lambda i, j, kk: (kk, j))],
        out_specs=pl.BlockSpec((tm, tn), lambda i, j, kk: (i, j)),
        out_shape=jax.ShapeDtypeStruct((m, n), out_dtype),
        scratch_shapes=scratch,
        compiler_params=_compiler_params(("parallel", "parallel", "arbitrary")),
        name="matmul",
    )(a, b)


def _col_slices(tn):
    cs = min(tn, MXU_DIM)
    return [slice(c, c + cs) for c in range(0, tn, cs)]


def _mm_slab_kernel(a_ref, b_ref, o_ref):
    acc = _dot(a_ref[...], b_ref[...])
    cb = o_ref.shape[2]
    for sidx in range(o_ref.shape[0]):
        o_ref[sidx] = acc[:, sidx * cb:(sidx + 1) * cb].astype(o_ref.dtype)


def _matmul_slabs(a, w, cb, out_dtype, tm=1024, tn=1024):
    m, k = a.shape
    n = w.shape[1]
    tm, tn = _pick_tile(m, tm), _pick_tile(n, tn)
    assert tn % cb == 0
    spb = tn // cb
    return pl.pallas_call(
        _mm_slab_kernel,
        grid=(m // tm, n // tn),
        in_specs=[pl.BlockSpec((tm, k), lambda i, j: (i, 0)),
                  pl.BlockSpec((k, tn), lambda i, j: (0, j))],
        out_specs=pl.BlockSpec((spb, tm, cb), lambda i, j: (j, i, 0)),
        out_shape=jax.ShapeDtypeStruct((n // cb, m, cb), out_dtype),
        compiler_params=_compiler_params(("parallel", "parallel")),
        name="matmul_slabs",
    )(a, w)


def _mm_glu_kernel(*refs, nslab):
    a_refs, (bv_ref, bg_ref, o_ref) = refs[:nslab], refs[nslab:]
    a = jnp.concatenate([r[...] for r in a_refs], axis=1)
    for sl in _col_slices(o_ref.shape[1]):
        v = _dot(a, bv_ref[:, sl])
        g = _dot(a, bg_ref[:, sl])
        o_ref[:, sl] = (v * _sigmoid(g)).astype(o_ref.dtype)


def _matmul_glu(a_slabs, w, layer, out_dtype, tm=1024, tn=1024):
    nslab, m, cb = a_slabs.shape
    k = nslab * cb
    n = w.shape[-1] // 2
    tm, tn = _pick_tile(m, tm), _pick_tile(n, tn)
    nj = n // tn
    slab_specs = [pl.BlockSpec((None, tm, cb), functools.partial(lambda i, j, sidx: (sidx, i, 0), sidx=sidx))
                  for sidx in range(nslab)]
    return pl.pallas_call(
        functools.partial(_mm_glu_kernel, nslab=nslab),
        grid=(m // tm, nj),
        in_specs=slab_specs + [_weight_spec(w, layer, (k, tn), lambda i, j: (0, j)),
                               _weight_spec(w, layer, (k, tn), lambda i, j: (0, j + nj))],
        out_specs=pl.BlockSpec((tm, tn), lambda i, j: (i, j)),
        out_shape=jax.ShapeDtypeStruct((m, n), out_dtype),
        compiler_params=_compiler_params(("parallel", "parallel")),
        name="matmul_glu",
    )(*([a_slabs] * nslab), w, w)


def _mm_mix_kernel(a_ref, b_ref, g1_ref, g2_ref, y2_ref, o_ref):
    a = a_ref[...]
    for sl in _col_slices(o_ref.shape[1]):
        y1 = _dot(a, b_ref[:, sl])
        o_ref[:, sl] = (_sigmoid(g1_ref[:, sl].astype(F32)) * y1
                        + _sigmoid(g2_ref[:, sl].astype(F32)) * y2_ref[:, sl].astype(F32)
                        ).astype(o_ref.dtype)


def _matmul_mix(a, w, layer, proj, gate_col, y2, out_dtype, tm=1024):
    m, k = a.shape
    n = w.shape[-1]
    tm = _pick_tile(m, tm)
    tn = math.gcd(math.gcd(512, gate_col), n)
    nj = n // tn
    g0 = gate_col // tn
    return pl.pallas_call(
        _mm_mix_kernel,
        grid=(m // tm, nj),
        in_specs=[pl.BlockSpec((tm, k), lambda i, j: (i, 0)),
                  _weight_spec(w, layer, (k, tn), lambda i, j: (0, j)),
                  pl.BlockSpec((tm, tn), lambda i, j: (i, g0 + j)),
                  pl.BlockSpec((tm, tn), lambda i, j: (i, g0 + nj + j)),
                  pl.BlockSpec((tm, tn), lambda i, j: (i, j))],
        out_specs=pl.BlockSpec((tm, tn), lambda i, j: (i, j)),
        out_shape=jax.ShapeDtypeStruct((m, n), out_dtype),
        compiler_params=_compiler_params(("parallel", "parallel")),
        name="matmul_mix",
    )(a, w, proj, proj, y2)


def _mm_resid_kernel(a_ref, b_ref, x_ref, g_ref, o_ref, *scratch, nk):
    if nk == 1:
        a = a_ref[...]
        for sl in _col_slices(o_ref.shape[1]):
            o_ref[:, sl] = x_ref[:, sl] + g_ref[:, sl] * _dot(a, b_ref[:, sl])
        return
    acc_ref, = scratch
    k = pl.program_id(2)

    @pl.when(k == 0)
    def _():
        acc_ref[...] = jnp.zeros_like(acc_ref)

    acc_ref[...] += _dot(a_ref[...], b_ref[...])

    @pl.when(k == nk - 1)
    def _():
        o_ref[...] = x_ref[...] + g_ref[...] * acc_ref[...]


def _matmul_resid(a, w, layer, x, gate, tm=1024, tk=4096):
    m, k = a.shape
    n = w.shape[-1]
    tk, nk = _k_tiling(k, tk)
    tm, tn = _pick_tile(m, tm), _pick_tile(n, 1024 if nk > 1 else 512)
    scratch = [pltpu.VMEM((tm, tn), F32)] if nk > 1 else []
    return pl.pallas_call(
        functools.partial(_mm_resid_kernel, nk=nk),
        grid=(m // tm, n // tn, nk),
        in_specs=[pl.BlockSpec((tm, tk), lambda i, j, kk: (i, kk)),
                  _weight_spec(w, layer, (tk, tn), lambda i, j, kk: (kk, j)),
                  pl.BlockSpec((tm, tn), lambda i, j, kk: (i, j)),
                  pl.BlockSpec((1, tn), lambda i, j, kk: (0, j))],
        out_specs=pl.BlockSpec((tm, tn), lambda i, j, kk: (i, j)),
        out_shape=jax.ShapeDtypeStruct((m, n), F32),
        scratch_shapes=scratch,
        compiler_params=_compiler_params(("parallel", "parallel", "arbitrary")),
        name="matmul_resid",
    )(a, w, x, gate.reshape(1, n))


def _conv_taps(buf_ref, w_ref, b_ref, base, rows, sl):
    k = w_ref.shape[0]
    out = b_ref[:, sl] + w_ref[k - 1:k, sl] * buf_ref[pl.ds(base, rows), sl]
    for j in range(1, k):
        out = out + w_ref[k - 1 - j:k - j, sl] * buf_ref[pl.ds(base - j, rows), sl]
    return out


def _mm_convgate_kernel(a_ref, ba_ref, bv_ref, wa_ref, wv_ref, ca_ref, cv_ref, o_ref,
                        ua_ref, uv_ref, halo_ref):
    i = pl.program_id(0)
    j = pl.program_id(1)
    tm = a_ref.shape[0]

    @pl.when(i == 0)
    def _():
        halo_ref[j] = jnp.zeros(halo_ref.shape[1:], F32)

    ua_ref[0:HALO, :] = halo_ref[j, 0]
    uv_ref[0:HALO, :] = halo_ref[j, 1]
    a = a_ref[...]
    for sl in _col_slices(o_ref.shape[1]):
        ua_ref[HALO:, sl] = _dot(a, ba_ref[:, sl])
        uv_ref[HALO:, sl] = _dot(a, bv_ref[:, sl])
        ca = _conv_taps(ua_ref, wa_ref, ca_ref, HALO, tm, sl)
        cv = _conv_taps(uv_ref, wv_ref, cv_ref, HALO, tm, sl)
        o_ref[:, sl] = (ca * _sigmoid(ca) * cv).astype(o_ref.dtype)
    halo_ref[j, 0] = ua_ref[tm:tm + HALO, :]
    halo_ref[j, 1] = uv_ref[tm:tm + HALO, :]


def _matmul_convgate(a, w, layer, conv_w, conv_b, out_dtype, tm=1024):
    m, k = a.shape
    n = w.shape[-1] // 2
    kc = conv_w.shape[0]
    assert kc - 1 <= HALO
    tm = _pick_tile(m, tm)
    tn = math.gcd(512, n)
    nj = n // tn
    conv_b = conv_b.reshape(1, 2 * n)
    return pl.pallas_call(
        _mm_convgate_kernel,
        grid=(m // tm, nj),
        in_specs=[pl.BlockSpec((tm, k), lambda i, j: (i, 0)),
                  _weight_spec(w, layer, (k, tn), lambda i, j: (0, j)),
                  _weight_spec(w, layer, (k, tn), lambda i, j: (0, j + nj)),
                  pl.BlockSpec((kc, tn), lambda i, j: (0, j)),
                  pl.BlockSpec((kc, tn), lambda i, j: (0, j + nj)),
                  pl.BlockSpec((1, tn), lambda i, j: (0, j)),
                  pl.BlockSpec((1, tn), lambda i, j: (0, j + nj))],
        out_specs=pl.BlockSpec((tm, tn), lambda i, j: (i, j)),
        out_shape=jax.ShapeDtypeStruct((m, n), out_dtype),
        scratch_shapes=[pltpu.VMEM((tm + HALO, tn), F32), pltpu.VMEM((tm + HALO, tn), F32),
                        pltpu.VMEM((nj, 2, HALO, tn), F32)],
        compiler_params=_compiler_params(("arbitrary", "arbitrary")),
        name="matmul_convgate",
    )(a, w, w, conv_w, conv_w, conv_b, conv_b)


def _norm_kernel(x_ref, w_ref, s_ref, o_ref):
    x = x_ref[...]
    ms = jnp.mean(x * x, axis=-1, keepdims=True)
    o_ref[...] = (x * lax.rsqrt(ms + EPS) * w_ref[...] + s_ref[...]).astype(o_ref.dtype)


def _norm_mod(x, w, shift, out_dtype, tr=256):
    s, d = x.shape
    tr = _pick_tile(s, tr)
    return pl.pallas_call(
        _norm_kernel,
        grid=(s // tr,),
        in_specs=[pl.BlockSpec((tr, d), lambda i: (i, 0)),
                  pl.BlockSpec((1, d), lambda i: (0, 0)),
                  pl.BlockSpec((1, d), lambda i: (0, 0))],
        out_specs=pl.BlockSpec((tr, d), lambda i: (i, 0)),
        out_shape=jax.ShapeDtypeStruct((s, d), out_dtype),
        compiler_params=_compiler_params(("parallel",)),
        name="norm_mod",
    )(x, w.reshape(1, d), shift.reshape(1, d))


def _split3(x):
    hi = x.astype(BF16)
    r1 = x - hi.astype(F32)
    mid = r1.astype(BF16)
    lo = (r1 - mid.astype(F32)).astype(BF16)
    return hi, mid, lo


def _dot01_left(t01, x):
    hi, mid, lo = _split3(x)
    return _dot(t01, hi) + _dot(t01, mid) + _dot(t01, lo)


def _dot01_right(x, t01):
    hi, mid, lo = _split3(x)
    return _dot(hi, t01) + _dot(mid, t01) + _dot(lo, t01)


def _softplus(x):
    return jnp.maximum(x, 0.0) + jnp.log1p(jnp.exp(-jnp.abs(x)))


def _ssd_kernel(x_ref, b_ref, c_ref, z_ref, dt_ref, dtT_ref, bias_ref, biasT_ref,
                wx_ref, wb_ref, wc_ref, cbx_ref, cbb_ref, cbc_ref,
                arow_ref, acol_ref, e_ref, dexp_ref, nw_ref,
                o_ref, state_ref, xbuf_ref, bbuf_ref, cbuf_ref, *, nchunk, hpg):
    q = CHUNK
    p = SSD_HEAD_DIM
    r = x_ref.shape[0]

    @pl.when(pl.program_id(1) == 0)
    def _():
        state_ref[...] = jnp.zeros_like(state_ref)
        xbuf_ref[0:HALO, :] = jnp.zeros((HALO, xbuf_ref.shape[1]), F32)
        bbuf_ref[0:HALO, :] = jnp.zeros((HALO, bbuf_ref.shape[1]), F32)
        cbuf_ref[0:HALO, :] = jnp.zeros((HALO, cbuf_ref.shape[1]), F32)

    xbuf_ref[HALO:, :] = x_ref[...].astype(F32)
    bbuf_ref[HALO:, :] = b_ref[...].astype(F32)
    cbuf_ref[HALO:, :] = c_ref[...].astype(F32)

    row = lax.broadcasted_iota(jnp.int32, (q, q), 0)
    col = lax.broadcasted_iota(jnp.int32, (q, q), 1)
    mask = row >= col
    tri = mask.astype(BF16)
    tri_t = (row <= col).astype(BF16)
    e = e_ref[...]
    dexp = dexp_ref[...]
    acol = acol_ref[...]
    nw = nw_ref[...]
    full = slice(None)

    def conv_silu(buf_ref, w_ref, cb_ref, base):
        v = _conv_taps(buf_ref, w_ref, cb_ref, base, q, full)
        return v * _sigmoid(v)

    for ci in range(nchunk):
        sl = pl.ds(ci * q, q)
        base = HALO + ci * q
        x = conv_silu(xbuf_ref, wx_ref, cbx_ref, base)
        b = conv_silu(bbuf_ref, wb_ref, cbb_ref, base).astype(BF16)
        c = conv_silu(cbuf_ref, wc_ref, cbc_ref, base).astype(BF16)
        dt = _softplus(dt_ref[sl, :] + bias_ref[...])
        dt_t = _softplus(dtT_ref[:, sl] + biasT_ref[...])
        dt_exp = _dot01_right(dt, e)
        cum_h = _dot01_left(tri, dt * arow_ref[...])
        cum = _dot01_right(cum_h, e)
        cum_t = _dot01_right(dt_t * acol, tri_t)
        cum_last = cum[q - 1:q, :]
        xdt = x * dt_exp
        cb = lax.dot_general(c, b, (((1,), (1,)), ((), ())), preferred_element_type=F32)
        st = state_ref[...]
        y_inter = _dot(c, st.astype(BF16)) * jnp.exp(cum)
        ys = []
        for h in range(hpg):
            seg = cum[:, h * p:h * p + 1] - cum_t[h:h + 1, :]
            dec = jnp.exp(jnp.where(mask, seg, -jnp.inf))
            w = (cb * dec).astype(BF16)
            ys.append(_dot(w, xdt[:, h * p:(h + 1) * p].astype(BF16)))
        y = jnp.concatenate(ys, axis=1) + y_inter + dexp * x
        dec_end = jnp.exp(cum_last - cum)
        b_t = b.astype(F32).T.astype(BF16)
        state_ref[...] = st * jnp.exp(cum_last) + _dot(b_t, (xdt * dec_end).astype(BF16))
        zf = z_ref[sl, :].astype(F32)
        gt = y * (zf * _sigmoid(zf))
        ms = jnp.mean(gt * gt, axis=-1, keepdims=True)
        o_ref[sl, :] = (gt * lax.rsqrt(ms + EPS) * nw).astype(o_ref.dtype)

    xbuf_ref[0:HALO, :] = xbuf_ref[r:r + HALO, :]
    bbuf_ref[0:HALO, :] = bbuf_ref[r:r + HALO, :]
    cbuf_ref[0:HALO, :] = cbuf_ref[r:r + HALO, :]


def _ssd_branch(proj, dt_raw, conv_w, conv_b, dt_bias, a_log, d, norm_w, inner, rows_per_step=1024):
    s = proj.shape[0]
    h = dt_bias.shape[0]
    g = SSD_GROUPS
    hpg = h // g
    gw = hpg * SSD_HEAD_DIM
    n = SSD_STATE
    r = _pick_tile(s, rows_per_step)
    kc = conv_w.shape[0]
    assert r % CHUNK == 0 and inner == g * gw and gw % LANES == 0 and h <= LANES and kc - 1 <= HALO
    a = -jnp.exp(a_log)
    dt_t = dt_raw[:, :h].T.reshape(g, hpg, s)
    bias = jnp.pad(dt_bias, (0, LANES - h)).reshape(1, LANES)
    bias_t = jnp.broadcast_to(dt_bias.reshape(g, hpg, 1), (g, hpg, CHUNK))
    arow = jnp.pad(a, (0, LANES - h)).reshape(1, LANES)
    dexp = jnp.repeat(d, SSD_HEAD_DIM).reshape(g, 1, gw)
    acol = jnp.broadcast_to(a.reshape(g, hpg, 1), (g, hpg, CHUNK))
    lane_head = jnp.arange(inner) // SSD_HEAD_DIM
    e = (jnp.arange(LANES)[:, None] == lane_head[None, :]).astype(BF16)
    e = e.reshape(LANES, g, gw).transpose(1, 0, 2)
    conv_b = conv_b.reshape(1, -1)
    xb0 = inner // gw
    bb0 = 2 * inner // n
    cb0 = bb0 + g
    wb0 = inner // n
    wc0 = wb0 + g
    per_g = lambda gi, t: (gi, 0, 0)
    return pl.pallas_call(
        functools.partial(_ssd_kernel, nchunk=r // CHUNK, hpg=hpg),
        grid=(g, s // r),
        in_specs=[pl.BlockSpec((r, gw), lambda gi, t: (t, xb0 + gi)),
                  pl.BlockSpec((r, n), lambda gi, t: (t, bb0 + gi)),
                  pl.BlockSpec((r, n), lambda gi, t: (t, cb0 + gi)),
                  pl.BlockSpec((r, gw), lambda gi, t: (t, gi)),
                  pl.BlockSpec((r, LANES), lambda gi, t: (t, 0)),
                  pl.BlockSpec((None, hpg, r), lambda gi, t: (gi, 0, t)),
                  pl.BlockSpec((1, LANES), lambda gi, t: (0, 0)),
                  pl.BlockSpec((None, hpg, CHUNK), per_g),
                  pl.BlockSpec((kc, gw), lambda gi, t: (0, gi)),
                  pl.BlockSpec((kc, n), lambda gi, t: (0, wb0 + gi)),
                  pl.BlockSpec((kc, n), lambda gi, t: (0, wc0 + gi)),
                  pl.BlockSpec((1, gw), lambda gi, t: (0, gi)),
                  pl.BlockSpec((1, n), lambda gi, t: (0, wb0 + gi)),
                  pl.BlockSpec((1, n), lambda gi, t: (0, wc0 + gi)),
                  pl.BlockSpec((1, LANES), lambda gi, t: (0, 0)),
                  pl.BlockSpec((None, hpg, CHUNK), per_g),
                  pl.BlockSpec((None, LANES, gw), per_g),
                  pl.BlockSpec((None, 1, gw), per_g),
                  pl.BlockSpec((1, gw), lambda gi, t: (0, gi))],
        out_specs=pl.BlockSpec((r, gw), lambda gi, t: (t, gi)),
        out_shape=jax.ShapeDtypeStruct((s, inner), BF16),
        scratch_shapes=[pltpu.VMEM((n, gw), F32), pltpu.VMEM((HALO + r, gw), F32),
                        pltpu.VMEM((HALO + r, n), F32), pltpu.VMEM((HALO + r, n), F32)],
        compiler_params=_compiler_params(("parallel", "arbitrary")),
        name="ssd_branch",
    )(proj, proj, proj, proj, dt_raw, dt_t, bias, bias_t, conv_w, conv_w, conv_w,
      conv_b, conv_b, conv_b, arow, acol, e, dexp, norm_w.reshape(1, inner))


S5_BLOCK = 8


def _cmul(ar, ai, br, bi):
    return ar * br - ai * bi, ar * bi + ai * br


def _log2(n):
    assert n > 0 and n & (n - 1) == 0
    return n.bit_length() - 1


def _s5_expand_weights(ab_ref, ac_ref, kc_ref, wb_ref, wc_ref, t_ref):
    rcb, cn = ab_ref.shape
    w2 = wb_ref.shape[1]
    cb = t_ref.shape[1]
    n5 = cn // 2
    gpb = w2 // cn
    ch = cb // gpb
    step = MXU_DIM
    iota = lambda shape, dim: lax.broadcasted_iota(jnp.int32, shape, dim)

    def group(idx, unit):
        return (idx >> _log2(unit)) & (gpb - 1)

    for c0 in range(0, w2, step):
        r = iota((cn, step), 0)
        j = iota((cn, step), 1) + c0
        onehot = ((r >> _log2(n5)) == (j >> _log2(gpb * n5))) & ((r & (n5 - 1)) == (j & (n5 - 1)))
        spread = _dot(ab_ref[...], onehot.astype(BF16))
        keep = group(iota((rcb, step), 0), ch) == group(iota((rcb, step), 1) + c0, n5)
        wb_ref[:, c0:c0 + step] = jnp.where(keep, spread, 0.0).astype(BF16)
    for r0 in range(0, w2, step):
        i = iota((step, cn), 0) + r0
        r = iota((step, cn), 1)
        onehot = ((r >> _log2(n5)) == (i >> _log2(gpb * n5))) & ((r & (n5 - 1)) == (i & (n5 - 1)))
        spread = _dot(onehot.astype(BF16), ac_ref[...])
        keep = group(iota((step, rcb), 0) + r0, n5) == group(iota((step, rcb), 1), ch)
        wc_ref[r0:r0 + step, :] = jnp.where(keep, spread, 0.0).astype(BF16)
    r = iota((kc_ref.shape[1], cb), 0)
    j = iota((kc_ref.shape[1], cb), 1)
    spread = _dot(kc_ref[...], (r == (j & (ch - 1))).astype(BF16))
    keep = group(iota((rcb, cb), 0), ch) == group(iota((rcb, cb), 1), ch)
    t_ref[...] = jnp.where(keep, spread, 0.0).astype(BF16)


def _s5_kernel(u_ref, ab_ref, ac_ref, kc_ref, k_ref, d_ref, o_ref,
               wb_ref, wc_ref, t_ref, h_ref, carry_ref):
    m, rcb = u_ref.shape
    cb = t_ref.shape[1]
    nr = rcb // cb
    w = h_ref.shape[1] // 2
    ns = SUBLANES

    @pl.when(pl.program_id(1) == 0)
    def _():
        carry_ref[...] = jnp.zeros_like(carry_ref)
        _s5_expand_weights(ab_ref, ac_ref, kc_ref, wb_ref, wc_ref, t_ref)

    h_ref[...] = _dot(u_ref[...], wb_ref[...])
    first_row = lax.broadcasted_iota(jnp.int32, (ns, w), 0) == 0

    def scan(r, carry):
        cr, ci = carry
        rows = pl.ds(pl.multiple_of(r * ns, ns), ns)
        xr = h_ref[rows, :w]
        xi = h_ref[rows, w:]
        for idx, shift in enumerate((1, 2, 4)):
            pr, pi = _cmul(k_ref[2 * idx], k_ref[2 * idx + 1],
                           pltpu.roll(xr, shift, 0), pltpu.roll(xi, shift, 0))
            xr, xi = xr + pr, xi + pi
        pr, pi = _cmul(k_ref[6], k_ref[7], cr, ci)
        xr, xi = xr + pr, xi + pi
        h_ref[rows, :w] = jnp.where(first_row, cr, pltpu.roll(xr, 1, 0))
        h_ref[rows, w:] = jnp.where(first_row, ci, pltpu.roll(xi, 1, 0))
        return xr[ns - 1:ns, :], xi[ns - 1:ns, :]

    cr, ci = lax.fori_loop(0, m // ns, scan, (carry_ref[0:1, :w], carry_ref[0:1, w:]))
    carry_ref[0:1, :w] = cr
    carry_ref[0:1, w:] = ci

    y_state = _dot(h_ref[...].astype(BF16), wc_ref[...])
    for qo in range(nr):
        lo = slice(qo * cb, (qo + 1) * cb)
        acc = y_state[:, lo] + d_ref[:, lo] * u_ref[:, lo].astype(F32)
        for qi in range(qo + 1):
            acc = acc + _dot(u_ref[:, qi * cb:(qi + 1) * cb], t_ref[(qo - qi) * cb:(qo - qi + 1) * cb, :])
        o_ref[:, lo] = jax.nn.gelu(acc).astype(o_ref.dtype)


def _s5_constants(lam_re, lam_im, log_dt, b_re, b_im, c_re, c_im):
    hi = lax.Precision.HIGHEST
    nl, g5, n5 = lam_re.shape
    ch = S5_GROUP_CH
    nr = S5_BLOCK
    gpb = min(S5_GROUPS_PER_BLOCK, g5)
    nblk = g5 // gpb
    dt = jnp.exp(log_dt)[..., None]
    mag = jnp.exp(lam_re * dt)
    ang = lam_im * dt
    abar_re, abar_im = mag * jnp.cos(ang), mag * jnp.sin(ang)
    den = lam_re * lam_re + lam_im * lam_im
    nre, nim = abar_re - 1.0, abar_im
    coef_re = (nre * lam_re + nim * lam_im) / den
    coef_im = (nim * lam_re - nre * lam_im) / den
    bbar_re = coef_re[..., None] * b_re - coef_im[..., None] * b_im
    bbar_im = coef_re[..., None] * b_im + coef_im[..., None] * b_re

    def powers(pr, pi, count):
        out = [(jnp.ones_like(pr), jnp.zeros_like(pi))]
        for _ in range(count):
            out.append(_cmul(out[-1][0], out[-1][1], pr, pi))
        return jnp.stack([o[0] for o in out], axis=1), jnp.stack([o[1] for o in out], axis=1)

    ap_re, ap_im = powers(abar_re, abar_im, nr)
    bp_re, bp_im = powers(ap_re[:, nr], ap_im[:, nr], SUBLANES)

    ab_re, ab_im = _cmul(ap_re[:, :nr, :, :, None], ap_im[:, :nr, :, :, None],
                         bbar_re[:, None], bbar_im[:, None])
    taps = (jnp.einsum('lgon,ldgni->ldgoi', c_re, ab_re, precision=hi)
            - jnp.einsum('lgon,ldgni->ldgoi', c_im, ab_im, precision=hi))
    kc = taps.reshape(nl, nr, nblk, gpb, ch, ch).transpose(0, 2, 1, 3, 5, 4)
    kc = jnp.pad(kc.reshape(nl, nblk, nr * gpb * ch, ch), ((0, 0), (0, 0), (0, 0), (0, LANES - ch)))

    ab = jnp.stack([ab_re[:, ::-1], ab_im[:, ::-1]], axis=2)
    ab = ab.reshape(nl, nr, 2, nblk, gpb, n5, ch).transpose(0, 3, 1, 4, 6, 2, 5)
    ab = ab.reshape(nl, nblk, nr * gpb * ch, 2 * n5)

    ca_re, ca_im = _cmul(c_re[:, None], c_im[:, None],
                         ap_re[:, 1:, :, None, :], ap_im[:, 1:, :, None, :])
    ac = jnp.stack([ca_re, -ca_im], axis=2)
    ac = ac.reshape(nl, nr, 2, nblk, gpb, ch, n5).transpose(0, 3, 2, 6, 1, 4, 5)
    ac = ac.reshape(nl, nblk, 2 * n5, nr * gpb * ch)

    def lanes(v):
        return v.reshape(nl, v.shape[1], nblk, gpb * n5).transpose(0, 2, 1, 3)

    rows = jnp.arange(SUBLANES)[:, None]
    kinds = []
    for shift in (1, 2, 4):
        for part in (bp_re, bp_im):
            kinds.append(jnp.where(rows >= shift, lanes(part[:, shift:shift + 1]), 0.0))
    for part in (bp_re, bp_im):
        kinds.append(lanes(part[:, 1:]))
    kconst = jnp.stack(kinds, axis=2)
    return ab.astype(BF16), ac.astype(BF16), kc.astype(BF16), kconst


def _s5_branch(u, consts, layer, d, blocks_per_step=512):
    nblk, s, cb = u.shape
    nr = S5_BLOCK
    ab, ac, kc, kconst = consts
    w2 = 2 * kconst.shape[4]
    assert s % nr == 0 and ab.shape[1:3] == (nblk, nr * cb)
    m = _pick_tile(s // nr, blocks_per_step)
    u2 = u.reshape(nblk, s // nr, nr * cb)
    d2 = jnp.tile(d.reshape(nblk, 1, cb), (1, 1, nr))
    out = pl.pallas_call(
        _s5_kernel,
        grid=(nblk, s // nr // m),
        in_specs=[pl.BlockSpec((None, m, nr * cb), lambda gb, t: (gb, t, 0)),
                  pl.BlockSpec((None, None) + ab.shape[2:], lambda gb, t: (layer, gb, 0, 0)),
                  pl.BlockSpec((None, None) + ac.shape[2:], lambda gb, t: (layer, gb, 0, 0)),
                  pl.BlockSpec((None, None) + kc.shape[2:], lambda gb, t: (layer, gb, 0, 0)),
                  pl.BlockSpec((None, None, 8, SUBLANES, w2 // 2), lambda gb, t: (layer, gb, 0, 0, 0)),
                  pl.BlockSpec((None, 1, nr * cb), lambda gb, t: (gb, 0, 0))],
        out_specs=pl.BlockSpec((None, m, nr * cb), lambda gb, t: (gb, t, 0)),
        out_shape=jax.ShapeDtypeStruct((nblk, s // nr, nr * cb), BF16),
        scratch_shapes=[pltpu.VMEM((nr * cb, w2), BF16), pltpu.VMEM((w2, nr * cb), BF16),
                        pltpu.VMEM((nr * cb, cb), BF16),
                        pltpu.VMEM((m, w2), F32), pltpu.VMEM((SUBLANES, w2), F32)],
        compiler_params=_compiler_params(("parallel", "arbitrary")),
        name="s5_branch",
    )(u2, ab, ac, kc, kconst, d2)
    return out.reshape(nblk, s, cb)


def kernel(x, c, ada_w1, ada_w2, ada_b, ada_table, norm_mix_w, w_in, ssd_conv_w, ssd_conv_b,
           ssd_dt_bias, ssd_a_log, ssd_d, ssd_norm_w, w_ssd_out, s5_lam_re, s5_lam_im, s5_log_dt,
           s5_b_re, s5_b_im, s5_c_re, s5_c_im, s5_d, w_glu, w_out, norm_ffn_w, w_up, ffn_conv_w,
           ffn_conv_b, w_down, final_norm_w):
    bsz, seq, dm = x.shape
    assert bsz == 1
    depth = w_in.shape[0]
    heads = ssd_dt_bias.shape[1]
    inner = heads * SSD_HEAD_DIM
    xbc_w = inner + 2 * SSD_GROUPS * SSD_STATE
    w5 = s5_d.shape[1]
    o_dt = inner + xbc_w
    o_u5 = o_dt + heads
    s5_cb = min(S5_GROUPS_PER_BLOCK, w5 // S5_GROUP_CH) * S5_GROUP_CH

    cpad = jnp.pad(jax.nn.silu(c), ((0, 2 * SUBLANES - bsz), (0, 0))).astype(BF16)
    t1 = _matmul(cpad, ada_w1.astype(BF16), BF16)
    mod = (_matmul(t1, ada_w2.astype(BF16), F32)[0] + ada_b).reshape(N_MOD, dm)

    s5_consts = _s5_constants(s5_lam_re, s5_lam_im, s5_log_dt, s5_b_re, s5_b_im, s5_c_re, s5_c_im)
    w_in_b, w_ssd_out_b, w_glu_b, w_out_b, w_up_b, w_down_b = (
        w.astype(BF16) for w in (w_in, w_ssd_out, w_glu, w_out, w_up, w_down))

    xs = x[0]
    for l in range(depth):
        m = mod + ada_table[l]
        h = _norm_mod(xs, norm_mix_w[l] * (1.0 + m[1]), m[0], BF16)
        w_dt = jnp.pad(w_in_b[l, :, o_dt:o_u5], ((0, 0), (0, LANES - heads)))
        proj = _matmul(h, w_in_b, BF16, layer=l, n=o_dt)
        gate_logits = _matmul(h, w_in_b[l, :, o_u5 + w5:], BF16)
        u5 = _matmul_slabs(h, w_in_b[l, :, o_u5:o_u5 + w5], s5_cb, BF16)
        dt_raw = _matmul(h, w_dt, F32)
        g_ssd = _ssd_branch(proj, dt_raw, ssd_conv_w[l], ssd_conv_b[l], ssd_dt_bias[l],
                            ssd_a_log[l], ssd_d[l], ssd_norm_w[l], inner)
        g_s5 = _s5_branch(u5, s5_consts, l, s5_d[l])
        y_s5 = _matmul_glu(g_s5, w_glu_b, l, BF16)
        mix = _matmul_mix(g_ssd, w_ssd_out_b, l, gate_logits, 0, y_s5, BF16)
        xs = _matmul_resid(mix, w_out_b, l, xs, m[2])
        h = _norm_mod(xs, norm_ffn_w[l] * (1.0 + m[4]), m[3], BF16)
        act = _matmul_convgate(h, w_up_b, l, ffn_conv_w[l], ffn_conv_b[l], BF16)
        xs = _matmul_resid(act, w_down_b, l, xs, m[5])
    out = _norm_mod(xs, final_norm_w, jnp.zeros_like(final_norm_w), F32)
    return out[None]
```

```python
import functools
import math

import jax
import jax.numpy as jnp
from jax import lax
from jax.experimental import pallas as pl
from jax.experimental.pallas import tpu as pltpu

F32 = jnp.float32
BF16 = jnp.bfloat16

SSD_HEAD_DIM = 64
SSD_GROUPS = 8
SSD_STATE = 128
CHUNK = 128
S5_GROUP_CH = 16
N_MOD = 6
EPS = 1e-6

V7X_VMEM_BYTES = 64 * 1024 * 1024
VMEM_LIMIT_BYTES = V7X_VMEM_BYTES - 8 * 1024 * 1024
SUBLANES = 8
LANES = 128
MXU_DIM = 256
S5_GROUPS_PER_BLOCK = MXU_DIM // S5_GROUP_CH
HALO = SUBLANES


def _compiler_params(semantics):
    return pltpu.CompilerParams(dimension_semantics=semantics,
                                vmem_limit_bytes=VMEM_LIMIT_BYTES)


def _pick_tile(dim, want):
    t = min(dim, want)
    while dim % t:
        t //= 2
    return t


def _dot(a, b):
    return jnp.dot(a, b, preferred_element_type=F32)


def _weight_spec(w, layer, block, index_map):
    if layer is None:
        assert w.ndim == 2
        return pl.BlockSpec(block, index_map)
    assert w.ndim == 3
    return pl.BlockSpec((None,) + block, lambda *idx: (layer,) + index_map(*idx))


def _sigmoid(x):
    return jax.nn.sigmoid(x)


def _mm_kernel(a_ref, b_ref, o_ref, *scratch, nk):
    if nk == 1:
        o_ref[...] = _dot(a_ref[...], b_ref[...]).astype(o_ref.dtype)
        return
    acc_ref, = scratch
    k = pl.program_id(2)

    @pl.when(k == 0)
    def _():
        acc_ref[...] = jnp.zeros_like(acc_ref)

    acc_ref[...] += _dot(a_ref[...], b_ref[...])

    @pl.when(k == nk - 1)
    def _():
        o_ref[...] = acc_ref[...].astype(o_ref.dtype)


def _k_tiling(k, tk):
    if k % tk:
        tk = k // 2 if k > tk else k
    tk = min(tk, k)
    assert k % tk == 0
    return tk, k // tk


def _matmul(a, b, out_dtype, tm=1024, tn=1024, tk=4096, layer=None, n=None):
    m, k = a.shape
    n = b.shape[-1] if n is None else n
    tm, tn = _pick_tile(m, tm), _pick_tile(n, tn)
    tk, nk = _k_tiling(k, tk)
    scratch = [pltpu.VMEM((tm, tn), F32)] if nk > 1 else []
    return pl.pallas_call(
        functools.partial(_mm_kernel, nk=nk),
        grid=(m // tm, n // tn, nk),
        in_specs=[pl.BlockSpec((tm, tk), lambda i, j, kk: (i, kk)),
                  _weight_spec(b, layer, (tk, tn), lambda i, j, kk: (kk, j))],
        out_specs=pl.BlockSpec((tm, tn), lambda i, j, kk: (i, j)),
        out_shape=jax.ShapeDtypeStruct((m, n), out_dtype),
        scratch_shapes=scratch,
        compiler_params=_compiler_params(("parallel", "parallel", "arbitrary")),
        name="matmul",
    )(a, b)


def _cast_kernel(x_ref, o_ref):
    o_ref[...] = x_ref[...].astype(o_ref.dtype)


def _cast_rows(w, out_dtype, tr=128):
    nl, k, n = w.shape
    rows = nl * k
    tr = _pick_tile(rows, tr)
    out = pl.pallas_call(
        _cast_kernel,
        grid=(rows // tr,),
        in_specs=[pl.BlockSpec((tr, n), lambda i: (i, 0))],
        out_specs=pl.BlockSpec((tr, n), lambda i: (i, 0)),
        out_shape=jax.ShapeDtypeStruct((rows, n), out_dtype),
        compiler_params=_compiler_params(("parallel",)),
        name="cast_rows",
    )(w.reshape(rows, n))
    return out.reshape(nl, k, n)


def _col_slices(tn):
    cs = min(tn, MXU_DIM)
    return [slice(c, c + cs) for c in range(0, tn, cs)]


def _mm_slab_kernel(a_ref, b_ref, o_ref):
    acc = _dot(a_ref[...], b_ref[...])
    cb = o_ref.shape[2]
    for sidx in range(o_ref.shape[0]):
        o_ref[sidx] = acc[:, sidx * cb:(sidx + 1) * cb].astype(o_ref.dtype)


def _matmul_slabs(a, w, cb, out_dtype, tm=1024, tn=1024):
    m, k = a.shape
    n = w.shape[1]
    tm, tn = _pick_tile(m, tm), _pick_tile(n, tn)
    assert tn % cb == 0
    spb = tn // cb
    return pl.pallas_call(
        _mm_slab_kernel,
        grid=(m // tm, n // tn),
        in_specs=[pl.BlockSpec((tm, k), lambda i, j: (i, 0)),
                  pl.BlockSpec((k, tn), lambda i, j: (0, j))],
        out_specs=pl.BlockSpec((spb, tm, cb), lambda i, j: (j, i, 0)),
        out_shape=jax.ShapeDtypeStruct((n // cb, m, cb), out_dtype),
        compiler_params=_compiler_params(("parallel", "parallel")),
        name="matmul_slabs",
    )(a, w)


def _mm_glu_kernel(*refs, nslab):
    a_refs, (bv_ref, bg_ref, o_ref) = refs[:nslab], refs[nslab:]
    a = jnp.concatenate([r[...] for r in a_refs], axis=1)
    for sl in _col_slices(o_ref.shape[1]):
        v = _dot(a, bv_ref[:, sl])
        g = _dot(a, bg_ref[:, sl])
        o_ref[:, sl] = (v * _sigmoid(g)).astype(o_ref.dtype)


def _matmul_glu(a_slabs, w, layer, out_dtype, tm=1024, tn=1024):
    nslab, m, cb = a_slabs.shape
    k = nslab * cb
    n = w.shape[-1] // 2
    tm, tn = _pick_tile(m, tm), _pick_tile(n, tn)
    nj = n // tn
    slab_specs = [pl.BlockSpec((None, tm, cb), functools.partial(lambda i, j, sidx: (sidx, i, 0), sidx=sidx))
                  for sidx in range(nslab)]
    return pl.pallas_call(
        functools.partial(_mm_glu_kernel, nslab=nslab),
        grid=(m // tm, nj),
        in_specs=slab_specs + [_weight_spec(w, layer, (k, tn), lambda i, j: (0, j)),
                               _weight_spec(w, layer, (k, tn), lambda i, j: (0, j + nj))],
        out_specs=pl.BlockSpec((tm, tn), lambda i, j: (i, j)),
        out_shape=jax.ShapeDtypeStruct((m, n), out_dtype),
        compiler_params=_compiler_params(("parallel", "parallel")),
        name="matmul_glu",
    )(*([a_slabs] * nslab), w, w)


def _mm_mix_kernel(a_ref, b_ref, g1_ref, g2_ref, y2_ref, o_ref):
    a = a_ref[...]
    for sl in _col_slices(o_ref.shape[1]):
        y1 = _dot(a, b_ref[:, sl])
        o_ref[:, sl] = (_sigmoid(g1_ref[:, sl].astype(F32)) * y1
                        + _sigmoid(g2_ref[:, sl].astype(F32)) * y2_ref[:, sl].astype(F32)
                        ).astype(o_ref.dtype)


def _matmul_mix(a, w, layer, proj, gate_col, y2, out_dtype, tm=1024):
    m, k = a.shape
    n = w.shape[-1]
    tm = _pick_tile(m, tm)
    tn = math.gcd(math.gcd(512, gate_col), n)
    nj = n // tn
    g0 = gate_col // tn
    return pl.pallas_call(
        _mm_mix_kernel,
        grid=(m // tm, nj),
        in_specs=[pl.BlockSpec((tm, k), lambda i, j: (i, 0)),
                  _weight_spec(w, layer, (k, tn), lambda i, j: (0, j)),
                  pl.BlockSpec((tm, tn), lambda i, j: (i, g0 + j)),
                  pl.BlockSpec((tm, tn), lambda i, j: (i, g0 + nj + j)),
                  pl.BlockSpec((tm, tn), lambda i, j: (i, j))],
        out_specs=pl.BlockSpec((tm, tn), lambda i, j: (i, j)),
        out_shape=jax.ShapeDtypeStruct((m, n), out_dtype),
        compiler_params=_compiler_params(("parallel", "parallel")),
        name="matmul_mix",
    )(a, w, proj, proj, y2)


def _mm_resid_kernel(a_ref, b_ref, x_ref, g_ref, o_ref, *scratch, nk):
    if nk == 1:
        a = a_ref[...]
        for sl in _col_slices(o_ref.shape[1]):
            o_ref[:, sl] = x_ref[:, sl] + g_ref[:, sl] * _dot(a, b_ref[:, sl])
        return
    acc_ref, = scratch
    k = pl.program_id(2)

    @pl.when(k == 0)
    def _():
        acc_ref[...] = jnp.zeros_like(acc_ref)

    acc_ref[...] += _dot(a_ref[...], b_ref[...])

    @pl.when(k == nk - 1)
    def _():
        o_ref[...] = x_ref[...] + g_ref[...] * acc_ref[...]


def _matmul_resid(a, w, layer, x, gate, tm=1024, tk=4096):
    m, k = a.shape
    n = w.shape[-1]
    tk, nk = _k_tiling(k, tk)
    tm, tn = _pick_tile(m, tm), _pick_tile(n, 1024 if nk > 1 else 512)
    scratch = [pltpu.VMEM((tm, tn), F32)] if nk > 1 else []
    return pl.pallas_call(
        functools.partial(_mm_resid_kernel, nk=nk),
        grid=(m // tm, n // tn, nk),
        in_specs=[pl.BlockSpec((tm, tk), lambda i, j, kk: (i, kk)),
                  _weight_spec(w, layer, (tk, tn), lambda i, j, kk: (kk, j)),
                  pl.BlockSpec((tm, tn), lambda i, j, kk: (i, j)),
                  pl.BlockSpec((1, tn), lambda i, j, kk: (0, j))],
        out_specs=pl.BlockSpec((tm, tn), lambda i, j, kk: (i, j)),
        out_shape=jax.ShapeDtypeStruct((m, n), F32),
        scratch_shapes=scratch,
        compiler_params=_compiler_params(("parallel", "parallel", "arbitrary")),
        name="matmul_resid",
    )(a, w, x, gate.reshape(1, n))


def _conv_taps(buf_ref, w_ref, b_ref, base, rows, sl):
    k = w_ref.shape[0]
    out = b_ref[:, sl] + w_ref[k - 1:k, sl] * buf_ref[pl.ds(base, rows), sl]
    for j in range(1, k):
        out = out + w_ref[k - 1 - j:k - j, sl] * buf_ref[pl.ds(base - j, rows), sl]
    return out


def _mm_convgate_kernel(a_ref, ba_ref, bv_ref, wa_ref, wv_ref, ca_ref, cv_ref, o_ref,
                        ua_ref, uv_ref, halo_ref):
    i = pl.program_id(0)
    j = pl.program_id(1)
    tm = a_ref.shape[0]

    @pl.when(i == 0)
    def _():
        halo_ref[j] = jnp.zeros(halo_ref.shape[1:], F32)

    ua_ref[0:HALO, :] = halo_ref[j, 0]
    uv_ref[0:HALO, :] = halo_ref[j, 1]
    a = a_ref[...]
    for sl in _col_slices(o_ref.shape[1]):
        ua_ref[HALO:, sl] = _dot(a, ba_ref[:, sl])
        uv_ref[HALO:, sl] = _dot(a, bv_ref[:, sl])
        ca = _conv_taps(ua_ref, wa_ref, ca_ref, HALO, tm, sl)
        cv = _conv_taps(uv_ref, wv_ref, cv_ref, HALO, tm, sl)
        o_ref[:, sl] = (ca * _sigmoid(ca) * cv).astype(o_ref.dtype)
    halo_ref[j, 0] = ua_ref[tm:tm + HALO, :]
    halo_ref[j, 1] = uv_ref[tm:tm + HALO, :]


def _matmul_convgate(a, w, layer, conv_w, conv_b, out_dtype, tm=1024):
    m, k = a.shape
    n = w.shape[-1] // 2
    kc = conv_w.shape[0]
    assert kc - 1 <= HALO
    tm = _pick_tile(m, tm)
    tn = math.gcd(512, n)
    nj = n // tn
    conv_b = conv_b.reshape(1, 2 * n)
    return pl.pallas_call(
        _mm_convgate_kernel,
        grid=(m // tm, nj),
        in_specs=[pl.BlockSpec((tm, k), lambda i, j: (i, 0)),
                  _weight_spec(w, layer, (k, tn), lambda i, j: (0, j)),
                  _weight_spec(w, layer, (k, tn), lambda i, j: (0, j + nj)),
                  pl.BlockSpec((kc, tn), lambda i, j: (0, j)),
                  pl.BlockSpec((kc, tn), lambda i, j: (0, j + nj)),
                  pl.BlockSpec((1, tn), lambda i, j: (0, j)),
                  pl.BlockSpec((1, tn), lambda i, j: (0, j + nj))],
        out_specs=pl.BlockSpec((tm, tn), lambda i, j: (i, j)),
        out_shape=jax.ShapeDtypeStruct((m, n), out_dtype),
        scratch_shapes=[pltpu.VMEM((tm + HALO, tn), F32), pltpu.VMEM((tm + HALO, tn), F32),
                        pltpu.VMEM((nj, 2, HALO, tn), F32)],
        compiler_params=_compiler_params(("arbitrary", "arbitrary")),
        name="matmul_convgate",
    )(a, w, w, conv_w, conv_w, conv_b, conv_b)


def _norm_kernel(x_ref, w_ref, s_ref, o_ref):
    x = x_ref[...]
    ms = jnp.mean(x * x, axis=-1, keepdims=True)
    o_ref[...] = (x * lax.rsqrt(ms + EPS) * w_ref[...] + s_ref[...]).astype(o_ref.dtype)


def _norm_mod(x, w, shift, out_dtype, tr=256):
    s, d = x.shape
    tr = _pick_tile(s, tr)
    return pl.pallas_call(
        _norm_kernel,
        grid=(s // tr,),
        in_specs=[pl.BlockSpec((tr, d), lambda i: (i, 0)),
                  pl.BlockSpec((1, d), lambda i: (0, 0)),
                  pl.BlockSpec((1, d), lambda i: (0, 0))],
        out_specs=pl.BlockSpec((tr, d), lambda i: (i, 0)),
        out_shape=jax.ShapeDtypeStruct((s, d), out_dtype),
        compiler_params=_compiler_params(("parallel",)),
        name="norm_mod",
    )(x, w.reshape(1, d), shift.reshape(1, d))


def _split3(x):
    hi = x.astype(BF16)
    r1 = x - hi.astype(F32)
    mid = r1.astype(BF16)
    lo = (r1 - mid.astype(F32)).astype(BF16)
    return hi, mid, lo


def _dot01_left(t01, x):
    hi, mid, lo = _split3(x)
    return _dot(t01, hi) + _dot(t01, mid) + _dot(t01, lo)


def _dot01_right(x, t01):
    hi, mid, lo = _split3(x)
    return _dot(hi, t01) + _dot(mid, t01) + _dot(lo, t01)


def _softplus(x):
    return jnp.maximum(x, 0.0) + jnp.log1p(jnp.exp(-jnp.abs(x)))


def _ssd_kernel(x_ref, b_ref, c_ref, z_ref, dt_ref, dtT_ref, bias_ref, biasT_ref,
                wx_ref, wb_ref, wc_ref, cbx_ref, cbb_ref, cbc_ref,
                arow_ref, acol_ref, e_ref, dexp_ref, nw_ref,
                o_ref, state_ref, xbuf_ref, bbuf_ref, cbuf_ref, *, nchunk, hpg):
    q = CHUNK
    p = SSD_HEAD_DIM
    r = x_ref.shape[0]

    @pl.when(pl.program_id(1) == 0)
    def _():
        state_ref[...] = jnp.zeros_like(state_ref)
        xbuf_ref[0:HALO, :] = jnp.zeros((HALO, xbuf_ref.shape[1]), F32)
        bbuf_ref[0:HALO, :] = jnp.zeros((HALO, bbuf_ref.shape[1]), F32)
        cbuf_ref[0:HALO, :] = jnp.zeros((HALO, cbuf_ref.shape[1]), F32)

    xbuf_ref[HALO:, :] = x_ref[...].astype(F32)
    bbuf_ref[HALO:, :] = b_ref[...].astype(F32)
    cbuf_ref[HALO:, :] = c_ref[...].astype(F32)

    row = lax.broadcasted_iota(jnp.int32, (q, q), 0)
    col = lax.broadcasted_iota(jnp.int32, (q, q), 1)
    mask = row >= col
    tri = mask.astype(BF16)
    tri_t = (row <= col).astype(BF16)
    e = e_ref[...]
    dexp = dexp_ref[...]
    acol = acol_ref[...]
    nw = nw_ref[...]
    full = slice(None)

    def conv_silu(buf_ref, w_ref, cb_ref, base):
        v = _conv_taps(buf_ref, w_ref, cb_ref, base, q, full)
        return v * _sigmoid(v)

    for ci in range(nchunk):
        sl = pl.ds(ci * q, q)
        base = HALO + ci * q
        x = conv_silu(xbuf_ref, wx_ref, cbx_ref, base)
        b = conv_silu(bbuf_ref, wb_ref, cbb_ref, base).astype(BF16)
        c = conv_silu(cbuf_ref, wc_ref, cbc_ref, base).astype(BF16)
        dt = _softplus(dt_ref[sl, :] + bias_ref[...])
        dt_t = _softplus(dtT_ref[:, sl] + biasT_ref[...])
        dt_exp = _dot01_right(dt, e)
        cum_h = _dot01_left(tri, dt * arow_ref[...])
        cum = _dot01_right(cum_h, e)
        cum_t = _dot01_right(dt_t * acol, tri_t)
        cum_last = cum[q - 1:q, :]
        xdt = x * dt_exp
        cb = lax.dot_general(c, b, (((1,), (1,)), ((), ())), preferred_element_type=F32)
        st = state_ref[...]
        y_inter = _dot(c, st.astype(BF16)) * jnp.exp(cum)
        ys = []
        for h in range(hpg):
            seg = cum[:, h * p:h * p + 1] - cum_t[h:h + 1, :]
            dec = jnp.exp(jnp.where(mask, seg, -jnp.inf))
            w = (cb * dec).astype(BF16)
            ys.append(_dot(w, xdt[:, h * p:(h + 1) * p].astype(BF16)))
        y = jnp.concatenate(ys, axis=1) + y_inter + dexp * x
        dec_end = jnp.exp(cum_last - cum)
        b_t = b.astype(F32).T.astype(BF16)
        state_ref[...] = st * jnp.exp(cum_last) + _dot(b_t, (xdt * dec_end).astype(BF16))
        zf = z_ref[sl, :].astype(F32)
        gt = y * (zf * _sigmoid(zf))
        ms = jnp.mean(gt * gt, axis=-1, keepdims=True)
        o_ref[sl, :] = (gt * lax.rsqrt(ms + EPS) * nw).astype(o_ref.dtype)

    xbuf_ref[0:HALO, :] = xbuf_ref[r:r + HALO, :]
    bbuf_ref[0:HALO, :] = bbuf_ref[r:r + HALO, :]
    cbuf_ref[0:HALO, :] = cbuf_ref[r:r + HALO, :]


def _ssd_branch(proj, dt_raw, conv_w, conv_b, dt_bias, a_log, d, norm_w, inner, rows_per_step=1024):
    s = proj.shape[0]
    h = dt_bias.shape[0]
    g = SSD_GROUPS
    hpg = h // g
    gw = hpg * SSD_HEAD_DIM
    n = SSD_STATE
    r = _pick_tile(s, rows_per_step)
    kc = conv_w.shape[0]
    assert r % CHUNK == 0 and inner == g * gw and gw % LANES == 0 and h <= LANES and kc - 1 <= HALO
    a = -jnp.exp(a_log)
    dt_t = dt_raw[:, :h].T.reshape(g, hpg, s)
    bias = jnp.pad(dt_bias, (0, LANES - h)).reshape(1, LANES)
    bias_t = jnp.broadcast_to(dt_bias.reshape(g, hpg, 1), (g, hpg, CHUNK))
    arow = jnp.pad(a, (0, LANES - h)).reshape(1, LANES)
    dexp = jnp.repeat(d, SSD_HEAD_DIM).reshape(g, 1, gw)
    acol = jnp.broadcast_to(a.reshape(g, hpg, 1), (g, hpg, CHUNK))
    lane_head = jnp.arange(inner) // SSD_HEAD_DIM
    e = (jnp.arange(LANES)[:, None] == lane_head[None, :]).astype(BF16)
    e = e.reshape(LANES, g, gw).transpose(1, 0, 2)
    conv_b = conv_b.reshape(1, -1)
    xb0 = inner // gw
    bb0 = 2 * inner // n
    cb0 = bb0 + g
    wb0 = inner // n
    wc0 = wb0 + g
    per_g = lambda gi, t: (gi, 0, 0)
    return pl.pallas_call(
        functools.partial(_ssd_kernel, nchunk=r // CHUNK, hpg=hpg),
        grid=(g, s // r),
        in_specs=[pl.BlockSpec((r, gw), lambda gi, t: (t, xb0 + gi)),
                  pl.BlockSpec((r, n), lambda gi, t: (t, bb0 + gi)),
                  pl.BlockSpec((r, n), lambda gi, t: (t, cb0 + gi)),
                  pl.BlockSpec((r, gw), lambda gi, t: (t, gi)),
                  pl.BlockSpec((r, LANES), lambda gi, t: (t, 0)),
                  pl.BlockSpec((None, hpg, r), lambda gi, t: (gi, 0, t)),
                  pl.BlockSpec((1, LANES), lambda gi, t: (0, 0)),
                  pl.BlockSpec((None, hpg, CHUNK), per_g),
                  pl.BlockSpec((kc, gw), lambda gi, t: (0, gi)),
                  pl.BlockSpec((kc, n), lambda gi, t: (0, wb0 + gi)),
                  pl.BlockSpec((kc, n), lambda gi, t: (0, wc0 + gi)),
                  pl.BlockSpec((1, gw), lambda gi, t: (0, gi)),
                  pl.BlockSpec((1, n), lambda gi, t: (0, wb0 + gi)),
                  pl.BlockSpec((1, n), lambda gi, t: (0, wc0 + gi)),
                  pl.BlockSpec((1, LANES), lambda gi, t: (0, 0)),
                  pl.BlockSpec((None, hpg, CHUNK), per_g),
                  pl.BlockSpec((None, LANES, gw), per_g),
                  pl.BlockSpec((None, 1, gw), per_g),
                  pl.BlockSpec((1, gw), lambda gi, t: (0, gi))],
        out_specs=pl.BlockSpec((r, gw), lambda gi, t: (t, gi)),
        out_shape=jax.ShapeDtypeStruct((s, inner), BF16),
        scratch_shapes=[pltpu.VMEM((n, gw), F32), pltpu.VMEM((HALO + r, gw), F32),
                        pltpu.VMEM((HALO + r, n), F32), pltpu.VMEM((HALO + r, n), F32)],
        compiler_params=_compiler_params(("parallel", "arbitrary")),
        name="ssd_branch",
    )(proj, proj, proj, proj, dt_raw, dt_t, bias, bias_t, conv_w, conv_w, conv_w,
      conv_b, conv_b, conv_b, arow, acol, e, dexp, norm_w.reshape(1, inner))


S5_BLOCK = 8


def _cmul(ar, ai, br, bi):
    return ar * br - ai * bi, ar * bi + ai * br


def _log2(n):
    assert n > 0 and n & (n - 1) == 0
    return n.bit_length() - 1


def _s5_expand_weights(ab_ref, ac_ref, kc_ref, wb_ref, wc_ref, t_ref):
    rcb, cn = ab_ref.shape
    w2 = wb_ref.shape[1]
    cb = t_ref.shape[1]
    n5 = cn // 2
    gpb = w2 // cn
    ch = cb // gpb
    step = MXU_DIM
    iota = lambda shape, dim: lax.broadcasted_iota(jnp.int32, shape, dim)

    def group(idx, unit):
        return (idx >> _log2(unit)) & (gpb - 1)

    for c0 in range(0, w2, step):
        r = iota((cn, step), 0)
        j = iota((cn, step), 1) + c0
        onehot = ((r >> _log2(n5)) == (j >> _log2(gpb * n5))) & ((r & (n5 - 1)) == (j & (n5 - 1)))
        spread = _dot(ab_ref[...], onehot.astype(BF16))
        keep = group(iota((rcb, step), 0), ch) == group(iota((rcb, step), 1) + c0, n5)
        wb_ref[:, c0:c0 + step] = jnp.where(keep, spread, 0.0).astype(BF16)
    for r0 in range(0, w2, step):
        i = iota((step, cn), 0) + r0
        r = iota((step, cn), 1)
        onehot = ((r >> _log2(n5)) == (i >> _log2(gpb * n5))) & ((r & (n5 - 1)) == (i & (n5 - 1)))
        spread = _dot(onehot.astype(BF16), ac_ref[...])
        keep = group(iota((step, rcb), 0) + r0, n5) == group(iota((step, rcb), 1), ch)
        wc_ref[r0:r0 + step, :] = jnp.where(keep, spread, 0.0).astype(BF16)
    r = iota((kc_ref.shape[1], cb), 0)
    j = iota((kc_ref.shape[1], cb), 1)
    spread = _dot(kc_ref[...], (r == (j & (ch - 1))).astype(BF16))
    keep = group(iota((rcb, cb), 0), ch) == group(iota((rcb, cb), 1), ch)
    t_ref[...] = jnp.where(keep, spread, 0.0).astype(BF16)


def _s5_kernel(u_ref, ab_ref, ac_ref, kc_ref, k_ref, d_ref, o_ref,
               wb_ref, wc_ref, t_ref, h_ref, carry_ref):
    m, rcb = u_ref.shape
    cb = t_ref.shape[1]
    nr = rcb // cb
    w = h_ref.shape[1] // 2
    ns = SUBLANES

    @pl.when(pl.program_id(1) == 0)
    def _():
        carry_ref[...] = jnp.zeros_like(carry_ref)
        _s5_expand_weights(ab_ref, ac_ref, kc_ref, wb_ref, wc_ref, t_ref)

    h_ref[...] = _dot(u_ref[...], wb_ref[...])
    first_row = lax.broadcasted_iota(jnp.int32, (ns, w), 0) == 0

    def scan(r, carry):
        cr, ci = carry
        rows = pl.ds(pl.multiple_of(r * ns, ns), ns)
        xr = h_ref[rows, :w]
        xi = h_ref[rows, w:]
        for idx, shift in enumerate((1, 2, 4)):
            pr, pi = _cmul(k_ref[2 * idx], k_ref[2 * idx + 1],
                           pltpu.roll(xr, shift, 0), pltpu.roll(xi, shift, 0))
            xr, xi = xr + pr, xi + pi
        pr, pi = _cmul(k_ref[6], k_ref[7], cr, ci)
        xr, xi = xr + pr, xi + pi
        h_ref[rows, :w] = jnp.where(first_row, cr, pltpu.roll(xr, 1, 0))
        h_ref[rows, w:] = jnp.where(first_row, ci, pltpu.roll(xi, 1, 0))
        return xr[ns - 1:ns, :], xi[ns - 1:ns, :]

    cr, ci = lax.fori_loop(0, m // ns, scan, (carry_ref[0:1, :w], carry_ref[0:1, w:]))
    carry_ref[0:1, :w] = cr
    carry_ref[0:1, w:] = ci

    y_state = _dot(h_ref[...].astype(BF16), wc_ref[...])
    for qo in range(nr):
        lo = slice(qo * cb, (qo + 1) * cb)
        acc = y_state[:, lo] + d_ref[:, lo] * u_ref[:, lo].astype(F32)
        for qi in range(qo + 1):
            acc = acc + _dot(u_ref[:, qi * cb:(qi + 1) * cb], t_ref[(qo - qi) * cb:(qo - qi + 1) * cb, :])
        o_ref[:, lo] = jax.nn.gelu(acc).astype(o_ref.dtype)


def _s5_constants(lam_re, lam_im, log_dt, b_re, b_im, c_re, c_im):
    hi = lax.Precision.HIGHEST
    nl, g5, n5 = lam_re.shape
    ch = S5_GROUP_CH
    nr = S5_BLOCK
    gpb = min(S5_GROUPS_PER_BLOCK, g5)
    nblk = g5 // gpb
    dt = jnp.exp(log_dt)[..., None]
    mag = jnp.exp(lam_re * dt)
    ang = lam_im * dt
    abar_re, abar_im = mag * jnp.cos(ang), mag * jnp.sin(ang)
    den = lam_re * lam_re + lam_im * lam_im
    nre, nim = abar_re - 1.0, abar_im
    coef_re = (nre * lam_re + nim * lam_im) / den
    coef_im = (nim * lam_re - nre * lam_im) / den
    bbar_re = coef_re[..., None] * b_re - coef_im[..., None] * b_im
    bbar_im = coef_re[..., None] * b_im + coef_im[..., None] * b_re

    def powers(pr, pi, count):
        out = [(jnp.ones_like(pr), jnp.zeros_like(pi))]
        for _ in range(count):
            out.append(_cmul(out[-1][0], out[-1][1], pr, pi))
        return jnp.stack([o[0] for o in out], axis=1), jnp.stack([o[1] for o in out], axis=1)

    ap_re, ap_im = powers(abar_re, abar_im, nr)
    bp_re, bp_im = powers(ap_re[:, nr], ap_im[:, nr], SUBLANES)

    ab_re, ab_im = _cmul(ap_re[:, :nr, :, :, None], ap_im[:, :nr, :, :, None],
                         bbar_re[:, None], bbar_im[:, None])
    taps = (jnp.einsum('lgon,ldgni->ldgoi', c_re, ab_re, precision=hi)
            - jnp.einsum('lgon,ldgni->ldgoi', c_im, ab_im, precision=hi))
    kc = taps.reshape(nl, nr, nblk, gpb, ch, ch).transpose(0, 2, 1, 3, 5, 4)
    kc = jnp.pad(kc.reshape(nl, nblk, nr * gpb * ch, ch), ((0, 0), (0, 0), (0, 0), (0, LANES - ch)))

    ab = jnp.stack([ab_re[:, ::-1], ab_im[:, ::-1]], axis=2)
    ab = ab.reshape(nl, nr, 2, nblk, gpb, n5, ch).transpose(0, 3, 1, 4, 6, 2, 5)
    ab = ab.reshape(nl, nblk, nr * gpb * ch, 2 * n5)

    ca_re, ca_im = _cmul(c_re[:, None], c_im[:, None],
                         ap_re[:, 1:, :, None, :], ap_im[:, 1:, :, None, :])
    ac = jnp.stack([ca_re, -ca_im], axis=2)
    ac = ac.reshape(nl, nr, 2, nblk, gpb, ch, n5).transpose(0, 3, 2, 6, 1, 4, 5)
    ac = ac.reshape(nl, nblk, 2 * n5, nr * gpb * ch)

    def lanes(v):
        return v.reshape(nl, v.shape[1], nblk, gpb * n5).transpose(0, 2, 1, 3)

    rows = jnp.arange(SUBLANES)[:, None]
    kinds = []
    for shift in (1, 2, 4):
        for part in (bp_re, bp_im):
            kinds.append(jnp.where(rows >= shift, lanes(part[:, shift:shift + 1]), 0.0))
    for part in (bp_re, bp_im):
        kinds.append(lanes(part[:, 1:]))
    kconst = jnp.stack(kinds, axis=2)
    return ab.astype(BF16), ac.astype(BF16), kc.astype(BF16), kconst


def _s5_branch(u, consts, layer, d, blocks_per_step=512):
    nblk, s, cb = u.shape
    nr = S5_BLOCK
    ab, ac, kc, kconst = consts
    w2 = 2 * kconst.shape[4]
    assert s % nr == 0 and ab.shape[1:3] == (nblk, nr * cb)
    m = _pick_tile(s // nr, blocks_per_step)
    u2 = u.reshape(nblk, s // nr, nr * cb)
    d2 = jnp.tile(d.reshape(nblk, 1, cb), (1, 1, nr))
    out = pl.pallas_call(
        _s5_kernel,
        grid=(nblk, s // nr // m),
        in_specs=[pl.BlockSpec((None, m, nr * cb), lambda gb, t: (gb, t, 0)),
                  pl.BlockSpec((None, None) + ab.shape[2:], lambda gb, t: (layer, gb, 0, 0)),
                  pl.BlockSpec((None, None) + ac.shape[2:], lambda gb, t: (layer, gb, 0, 0)),
                  pl.BlockSpec((None, None) + kc.shape[2:], lambda gb, t: (layer, gb, 0, 0)),
                  pl.BlockSpec((None, None, 8, SUBLANES, w2 // 2), lambda gb, t: (layer, gb, 0, 0, 0)),
                  pl.BlockSpec((None, 1, nr * cb), lambda gb, t: (gb, 0, 0))],
        out_specs=pl.BlockSpec((None, m, nr * cb), lambda gb, t: (gb, t, 0)),
        out_shape=jax.ShapeDtypeStruct((nblk, s // nr, nr * cb), BF16),
        scratch_shapes=[pltpu.VMEM((nr * cb, w2), BF16), pltpu.VMEM((w2, nr * cb), BF16),
                        pltpu.VMEM((nr * cb, cb), BF16),
                        pltpu.VMEM((m, w2), F32), pltpu.VMEM((SUBLANES, w2), F32)],
        compiler_params=_compiler_params(("parallel", "arbitrary")),
        name="s5_branch",
    )(u2, ab, ac, kc, kconst, d2)
    return out.reshape(nblk, s, cb)


def kernel(x, c, ada_w1, ada_w2, ada_b, ada_table, norm_mix_w, w_in, ssd_conv_w, ssd_conv_b,
           ssd_dt_bias, ssd_a_log, ssd_d, ssd_norm_w, w_ssd_out, s5_lam_re, s5_lam_im, s5_log_dt,
           s5_b_re, s5_b_im, s5_c_re, s5_c_im, s5_d, w_glu, w_out, norm_ffn_w, w_up, ffn_conv_w,
           ffn_conv_b, w_down, final_norm_w):
    bsz, seq, dm = x.shape
    assert bsz == 1
    depth = w_in.shape[0]
    heads = ssd_dt_bias.shape[1]
    inner = heads * SSD_HEAD_DIM
    xbc_w = inner + 2 * SSD_GROUPS * SSD_STATE
    w5 = s5_d.shape[1]
    o_dt = inner + xbc_w
    o_u5 = o_dt + heads
    s5_cb = min(S5_GROUPS_PER_BLOCK, w5 // S5_GROUP_CH) * S5_GROUP_CH

    cpad = jnp.pad(jax.nn.silu(c), ((0, 2 * SUBLANES - bsz), (0, 0))).astype(BF16)
    t1 = _matmul(cpad, ada_w1.astype(BF16), BF16)
    mod = (_matmul(t1, ada_w2.astype(BF16), F32)[0] + ada_b).reshape(N_MOD, dm)

    s5_consts = _s5_constants(s5_lam_re, s5_lam_im, s5_log_dt, s5_b_re, s5_b_im, s5_c_re, s5_c_im)
    w_ssd_out_b, w_glu_b, w_out_b, w_up_b, w_down_b = (
        w.astype(BF16) for w in (w_ssd_out, w_glu, w_out, w_up, w_down))
    w_in_b = _cast_rows(w_in, BF16)

    xs = x[0]
    for l in range(depth):
        m = mod + ada_table[l]
        h = _norm_mod(xs, norm_mix_w[l] * (1.0 + m[1]), m[0], BF16)
        w_dt = jnp.pad(w_in_b[l, :, o_dt:o_u5], ((0, 0), (0, LANES - heads)))
        proj = _matmul(h, w_in_b, BF16, layer=l, n=o_dt)
        gate_logits = _matmul(h, w_in_b[l, :, o_u5 + w5:], BF16)
        u5 = _matmul_slabs(h, w_in_b[l, :, o_u5:o_u5 + w5], s5_cb, BF16)
        dt_raw = _matmul(h, w_dt, F32)
        g_ssd = _ssd_branch(proj, dt_raw, ssd_conv_w[l], ssd_conv_b[l], ssd_dt_bias[l],
                            ssd_a_log[l], ssd_d[l], ssd_norm_w[l], inner)
        g_s5 = _s5_branch(u5, s5_consts, l, s5_d[l])
        y_s5 = _matmul_glu(g_s5, w_glu_b, l, BF16)
        mix = _matmul_mix(g_ssd, w_ssd_out_b, l, gate_logits, 0, y_s5, BF16)
        xs = _matmul_resid(mix, w_out_b, l, xs, m[2])
        h = _norm_mod(xs, norm_ffn_w[l] * (1.0 + m[4]), m[3], BF16)
        act = _matmul_convgate(h, w_up_b, l, ffn_conv_w[l], ffn_conv_b[l], BF16)
        xs = _matmul_resid(act, w_down_b, l, xs, m[5])
    out = _norm_mod(xs, final_norm_w, jnp.zeros_like(final_norm_w), F32)
    return out[None]
```

```python
import functools
import math

import jax
import jax.numpy as jnp
from jax import lax
from jax.experimental import pallas as pl
from jax.experimental.pallas import tpu as pltpu

F32 = jnp.float32
BF16 = jnp.bfloat16

SSD_HEAD_DIM = 64
SSD_GROUPS = 8
SSD_STATE = 128
CHUNK = 128
S5_GROUP_CH = 16
N_MOD = 6
EPS = 1e-6

V7X_VMEM_BYTES = 64 * 1024 * 1024
VMEM_LIMIT_BYTES = V7X_VMEM_BYTES - 8 * 1024 * 1024
SUBLANES = 8
LANES = 128
MXU_DIM = 256
S5_GROUPS_PER_BLOCK = MXU_DIM // S5_GROUP_CH
HALO = SUBLANES


def _compiler_params(semantics):
    return pltpu.CompilerParams(dimension_semantics=semantics,
                                vmem_limit_bytes=VMEM_LIMIT_BYTES)


def _pick_tile(dim, want):
    t = min(dim, want)
    while dim % t:
        t //= 2
    return t


def _dot(a, b):
    return jnp.dot(a, b, preferred_element_type=F32)


def _weight_spec(w, layer, block, index_map):
    if layer is None:
        assert w.ndim == 2
        return pl.BlockSpec(block, index_map)
    assert w.ndim == 3
    return pl.BlockSpec((None,) + block, lambda *idx: (layer,) + index_map(*idx))


def _sigmoid(x):
    return jax.nn.sigmoid(x)


def _mm_kernel(a_ref, b_ref, o_ref, *scratch, nk):
    if nk == 1:
        o_ref[...] = _dot(a_ref[...], b_ref[...]).astype(o_ref.dtype)
        return
    acc_ref, = scratch
    k = pl.program_id(2)

    @pl.when(k == 0)
    def _():
        acc_ref[...] = jnp.zeros_like(acc_ref)

    acc_ref[...] += _dot(a_ref[...], b_ref[...])

    @pl.when(k == nk - 1)
    def _():
        o_ref[...] = acc_ref[...].astype(o_ref.dtype)


def _k_tiling(k, tk):
    if k % tk:
        tk = k // 2 if k > tk else k
    tk = min(tk, k)
    assert k % tk == 0
    return tk, k // tk


def _matmul(a, b, out_dtype, tm=1024, tn=1024, tk=4096, layer=None, n=None):
    m, k = a.shape
    n = b.shape[-1] if n is None else n
    tm, tn = _pick_tile(m, tm), _pick_tile(n, tn)
    tk, nk = _k_tiling(k, tk)
    scratch = [pltpu.VMEM((tm, tn), F32)] if nk > 1 else []
    return pl.pallas_call(
        functools.partial(_mm_kernel, nk=nk),
        grid=(m // tm, n // tn, nk),
        in_specs=[pl.BlockSpec((tm, tk), lambda i, j, kk: (i, kk)),
                  _weight_spec(b, layer, (tk, tn), lambda i, j, kk: (kk, j))],
        out_specs=pl.BlockSpec((tm, tn), lambda i, j, kk: (i, j)),
        out_shape=jax.ShapeDtypeStruct((m, n), out_dtype),
        scratch_shapes=scratch,
        compiler_params=_compiler_params(("parallel", "parallel", "arbitrary")),
        name="matmul",
    )(a, b)


def _col_slices(tn):
    cs = min(tn, MXU_DIM)
    return [slice(c, c + cs) for c in range(0, tn, cs)]


def _mm_slab_kernel(a_ref, b_ref, o_ref):
    acc = _dot(a_ref[...], b_ref[...])
    cb = o_ref.shape[2]
    for sidx in range(o_ref.shape[0]):
        o_ref[sidx] = acc[:, sidx * cb:(sidx + 1) * cb].astype(o_ref.dtype)


def _matmul_slabs(a, w, cb, out_dtype, tm=1024, tn=1024):
    m, k = a.shape
    n = w.shape[1]
    tm, tn = _pick_tile(m, tm), _pick_tile(n, tn)
    assert tn % cb == 0
    spb = tn // cb
    return pl.pallas_call(
        _mm_slab_kernel,
        grid=(m // tm, n // tn),
        in_specs=[pl.BlockSpec((tm, k), lambda i, j: (i, 0)),
                  pl.BlockSpec((k, tn), lambda i, j: (0, j))],
        out_specs=pl.BlockSpec((spb, tm, cb), lambda i, j: (j, i, 0)),
        out_shape=jax.ShapeDtypeStruct((n // cb, m, cb), out_dtype),
        compiler_params=_compiler_params(("parallel", "parallel")),
        name="matmul_slabs",
    )(a, w)


def _mm_glu_kernel(*refs, nslab):
    a_refs, (bv_ref, bg_ref, o_ref) = refs[:nslab], refs[nslab:]
    a = jnp.concatenate([r[...] for r in a_refs], axis=1)
    for sl in _col_slices(o_ref.shape[1]):
        v = _dot(a, bv_ref[:, sl])
        g = _dot(a, bg_ref[:, sl])
        o_ref[:, sl] = (v * _sigmoid(g)).astype(o_ref.dtype)


def _matmul_glu(a_slabs, w, layer, out_dtype, tm=1024, tn=1024):
    nslab, m, cb = a_slabs.shape
    k = nslab * cb
    n = w.shape[-1] // 2
    tm, tn = _pick_tile(m, tm), _pick_tile(n, tn)
    nj = n // tn
    slab_specs = [pl.BlockSpec((None, tm, cb), functools.partial(lambda i, j, sidx: (sidx, i, 0), sidx=sidx))
                  for sidx in range(nslab)]
    return pl.pallas_call(
        functools.partial(_mm_glu_kernel, nslab=nslab),
        grid=(m // tm, nj),
        in_specs=slab_specs + [_weight_spec(w, layer, (k, tn), lambda i, j: (0, j)),
                               _weight_spec(w, layer, (k, tn), lambda i, j: (0, j + nj))],
        out_specs=pl.BlockSpec((tm, tn), lambda i, j: (i, j)),
        out_shape=jax.ShapeDtypeStruct((m, n), out_dtype),
        compiler_params=_compiler_params(("parallel", "parallel")),
        name="matmul_glu",
    )(*([a_slabs] * nslab), w, w)


def _mm_mix_kernel(a_ref, b_ref, g1_ref, g2_ref, y2_ref, o_ref):
    a = a_ref[...]
    for sl in _col_slices(o_ref.shape[1]):
        y1 = _dot(a, b_ref[:, sl])
        o_ref[:, sl] = (_sigmoid(g1_ref[:, sl].astype(F32)) * y1
                        + _sigmoid(g2_ref[:, sl].astype(F32)) * y2_ref[:, sl].astype(F32)
                        ).astype(o_ref.dtype)


def _matmul_mix(a, w, layer, proj, gate_col, y2, out_dtype, tm=1024):
    m, k = a.shape
    n = w.shape[-1]
    tm = _pick_tile(m, tm)
    tn = math.gcd(math.gcd(512, gate_col), n)
    nj = n // tn
    g0 = gate_col // tn
    return pl.pallas_call(
        _mm_mix_kernel,
        grid=(m // tm, nj),
        in_specs=[pl.BlockSpec((tm, k), lambda i, j: (i, 0)),
                  _weight_spec(w, layer, (k, tn), lambda i, j: (0, j)),
                  pl.BlockSpec((tm, tn), lambda i, j: (i, g0 + j)),
                  pl.BlockSpec((tm, tn), lambda i, j: (i, g0 + nj + j)),
                  pl.BlockSpec((tm, tn), lambda i, j: (i, j))],
        out_specs=pl.BlockSpec((tm, tn), lambda i, j: (i, j)),
        out_shape=jax.ShapeDtypeStruct((m, n), out_dtype),
        compiler_params=_compiler_params(("parallel", "parallel")),
        name="matmul_mix",
    )(a, w, proj, proj, y2)


def _mm_resid_kernel(a_ref, b_ref, x_ref, g_ref, o_ref, *scratch, nk):
    if nk == 1:
        a = a_ref[...]
        for sl in _col_slices(o_ref.shape[1]):
            o_ref[:, sl] = x_ref[:, sl] + g_ref[:, sl] * _dot(a, b_ref[:, sl])
        return
    acc_ref, = scratch
    k = pl.program_id(2)

    @pl.when(k == 0)
    def _():
        acc_ref[...] = jnp.zeros_like(acc_ref)

    acc_ref[...] += _dot(a_ref[...], b_ref[...])

    @pl.when(k == nk - 1)
    def _():
        o_ref[...] = x_ref[...] + g_ref[...] * acc_ref[...]


def _matmul_resid(a, w, layer, x, gate, tm=1024, tk=8192):
    m, k = a.shape
    n = w.shape[-1]
    tk, nk = _k_tiling(k, tk)
    tm, tn = _pick_tile(m, tm), _pick_tile(n, 1024 if nk > 1 else 512)
    scratch = [pltpu.VMEM((tm, tn), F32)] if nk > 1 else []
    return pl.pallas_call(
        functools.partial(_mm_resid_kernel, nk=nk),
        grid=(m // tm, n // tn, nk),
        in_specs=[pl.BlockSpec((tm, tk), lambda i, j, kk: (i, kk)),
                  _weight_spec(w, layer, (tk, tn), lambda i, j, kk: (kk, j)),
                  pl.BlockSpec((tm, tn), lambda i, j, kk: (i, j)),
                  pl.BlockSpec((1, tn), lambda i, j, kk: (0, j))],
        out_specs=pl.BlockSpec((tm, tn), lambda i, j, kk: (i, j)),
        out_shape=jax.ShapeDtypeStruct((m, n), F32),
        scratch_shapes=scratch,
        compiler_params=_compiler_params(("parallel", "parallel", "arbitrary")),
        name="matmul_resid",
    )(a, w, x, gate.reshape(1, n))


def _conv_taps(buf_ref, w_ref, b_ref, base, rows, sl):
    k = w_ref.shape[0]
    out = b_ref[:, sl] + w_ref[k - 1:k, sl] * buf_ref[pl.ds(base, rows), sl]
    for j in range(1, k):
        out = out + w_ref[k - 1 - j:k - j, sl] * buf_ref[pl.ds(base - j, rows), sl]
    return out


def _mm_convgate_kernel(a_ref, ba_ref, bv_ref, wa_ref, wv_ref, ca_ref, cv_ref, o_ref,
                        ua_ref, uv_ref, halo_ref):
    i = pl.program_id(0)
    j = pl.program_id(1)
    tm = a_ref.shape[0]

    @pl.when(i == 0)
    def _():
        halo_ref[j] = jnp.zeros(halo_ref.shape[1:], F32)

    ua_ref[0:HALO, :] = halo_ref[j, 0]
    uv_ref[0:HALO, :] = halo_ref[j, 1]
    a = a_ref[...]
    for sl in _col_slices(o_ref.shape[1]):
        ua_ref[HALO:, sl] = _dot(a, ba_ref[:, sl])
        uv_ref[HALO:, sl] = _dot(a, bv_ref[:, sl])
        ca = _conv_taps(ua_ref, wa_ref, ca_ref, HALO, tm, sl)
        cv = _conv_taps(uv_ref, wv_ref, cv_ref, HALO, tm, sl)
        o_ref[:, sl] = (ca * _sigmoid(ca) * cv).astype(o_ref.dtype)
    halo_ref[j, 0] = ua_ref[tm:tm + HALO, :]
    halo_ref[j, 1] = uv_ref[tm:tm + HALO, :]


def _matmul_convgate(a, w, layer, conv_w, conv_b, out_dtype, tm=1024):
    m, k = a.shape
    n = w.shape[-1] // 2
    kc = conv_w.shape[0]
    assert kc - 1 <= HALO
    tm = _pick_tile(m, tm)
    tn = math.gcd(512, n)
    nj = n // tn
    conv_b = conv_b.reshape(1, 2 * n)
    return pl.pallas_call(
        _mm_convgate_kernel,
        grid=(m // tm, nj),
        in_specs=[pl.BlockSpec((tm, k), lambda i, j: (i, 0)),
                  _weight_spec(w, layer, (k, tn), lambda i, j: (0, j)),
                  _weight_spec(w, layer, (k, tn), lambda i, j: (0, j + nj)),
                  pl.BlockSpec((kc, tn), lambda i, j: (0, j)),
                  pl.BlockSpec((kc, tn), lambda i, j: (0, j + nj)),
                  pl.BlockSpec((1, tn), lambda i, j: (0, j)),
                  pl.BlockSpec((1, tn), lambda i, j: (0, j + nj))],
        out_specs=pl.BlockSpec((tm, tn), lambda i, j: (i, j)),
        out_shape=jax.ShapeDtypeStruct((m, n), out_dtype),
        scratch_shapes=[pltpu.VMEM((tm + HALO, tn), F32), pltpu.VMEM((tm + HALO, tn), F32),
                        pltpu.VMEM((nj, 2, HALO, tn), F32)],
        compiler_params=_compiler_params(("arbitrary", "arbitrary")),
        name="matmul_convgate",
    )(a, w, w, conv_w, conv_w, conv_b, conv_b)


def _norm_kernel(x_ref, w_ref, s_ref, o_ref):
    x = x_ref[...]
    ms = jnp.mean(x * x, axis=-1, keepdims=True)
    o_ref[...] = (x * lax.rsqrt(ms + EPS) * w_ref[...] + s_ref[...]).astype(o_ref.dtype)


def _norm_mod(x, w, shift, out_dtype, tr=512):
    s, d = x.shape
    tr = _pick_tile(s, tr)
    return pl.pallas_call(
        _norm_kernel,
        grid=(s // tr,),
        in_specs=[pl.BlockSpec((tr, d), lambda i: (i, 0)),
                  pl.BlockSpec((1, d), lambda i: (0, 0)),
                  pl.BlockSpec((1, d), lambda i: (0, 0))],
        out_specs=pl.BlockSpec((tr, d), lambda i: (i, 0)),
        out_shape=jax.ShapeDtypeStruct((s, d), out_dtype),
        compiler_params=_compiler_params(("parallel",)),
        name="norm_mod",
    )(x, w.reshape(1, d), shift.reshape(1, d))


def _split3(x):
    hi = x.astype(BF16)
    r1 = x - hi.astype(F32)
    mid = r1.astype(BF16)
    lo = (r1 - mid.astype(F32)).astype(BF16)
    return hi, mid, lo


def _dot01_left(t01, x):
    hi, mid, lo = _split3(x)
    return _dot(t01, hi) + _dot(t01, mid) + _dot(t01, lo)


def _dot01_right(x, t01):
    hi, mid, lo = _split3(x)
    return _dot(hi, t01) + _dot(mid, t01) + _dot(lo, t01)


def _softplus(x):
    return jnp.maximum(x, 0.0) + jnp.log1p(jnp.exp(-jnp.abs(x)))


def _ssd_kernel(x_ref, b_ref, c_ref, z_ref, dt_ref, dtT_ref, bias_ref, biasT_ref,
                wx_ref, wb_ref, wc_ref, cbx_ref, cbb_ref, cbc_ref,
                arow_ref, acol_ref, e_ref, dexp_ref, nw_ref,
                o_ref, state_ref, xbuf_ref, bbuf_ref, cbuf_ref, *, nchunk, hpg):
    q = CHUNK
    p = SSD_HEAD_DIM
    r = x_ref.shape[0]

    @pl.when(pl.program_id(1) == 0)
    def _():
        state_ref[...] = jnp.zeros_like(state_ref)
        xbuf_ref[0:HALO, :] = jnp.zeros((HALO, xbuf_ref.shape[1]), F32)
        bbuf_ref[0:HALO, :] = jnp.zeros((HALO, bbuf_ref.shape[1]), F32)
        cbuf_ref[0:HALO, :] = jnp.zeros((HALO, cbuf_ref.shape[1]), F32)

    xbuf_ref[HALO:, :] = x_ref[...].astype(F32)
    bbuf_ref[HALO:, :] = b_ref[...].astype(F32)
    cbuf_ref[HALO:, :] = c_ref[...].astype(F32)

    row = lax.broadcasted_iota(jnp.int32, (q, q), 0)
    col = lax.broadcasted_iota(jnp.int32, (q, q), 1)
    mask = row >= col
    tri = mask.astype(BF16)
    tri_t = (row <= col).astype(BF16)
    e = e_ref[...]
    dexp = dexp_ref[...]
    acol = acol_ref[...]
    nw = nw_ref[...]
    full = slice(None)

    def conv_silu(buf_ref, w_ref, cb_ref, base):
        v = _conv_taps(buf_ref, w_ref, cb_ref, base, q, full)
        return v * _sigmoid(v)

    for ci in range(nchunk):
        sl = pl.ds(ci * q, q)
        base = HALO + ci * q
        x = conv_silu(xbuf_ref, wx_ref, cbx_ref, base)
        b = conv_silu(bbuf_ref, wb_ref, cbb_ref, base).astype(BF16)
        c = conv_silu(cbuf_ref, wc_ref, cbc_ref, base).astype(BF16)
        dt = _softplus(dt_ref[sl, :] + bias_ref[...])
        dt_t = _softplus(dtT_ref[:, sl] + biasT_ref[...])
        dt_exp = _dot01_right(dt, e)
        cum_h = _dot01_left(tri, dt * arow_ref[...])
        cum = _dot01_right(cum_h, e)
        cum_t = _dot01_right(dt_t * acol, tri_t)
        cum_last = cum[q - 1:q, :]
        xdt = x * dt_exp
        cb = lax.dot_general(c, b, (((1,), (1,)), ((), ())), preferred_element_type=F32)
        st = state_ref[...]
        y_inter = _dot(c, st.astype(BF16)) * jnp.exp(cum)
        ys = []
        for h in range(hpg):
            seg = cum[:, h * p:h * p + 1] - cum_t[h:h + 1, :]
            dec = jnp.exp(jnp.where(mask, seg, -jnp.inf))
            w = (cb * dec).astype(BF16)
            ys.append(_dot(w, xdt[:, h * p:(h + 1) * p].astype(BF16)))
        y = jnp.concatenate(ys, axis=1) + y_inter + dexp * x
        dec_end = jnp.exp(cum_last - cum)
        b_t = b.astype(F32).T.astype(BF16)
        state_ref[...] = st * jnp.exp(cum_last) + _dot(b_t, (xdt * dec_end).astype(BF16))
        zf = z_ref[sl, :].astype(F32)
        gt = y * (zf * _sigmoid(zf))
        ms = jnp.mean(gt * gt, axis=-1, keepdims=True)
        o_ref[sl, :] = (gt * lax.rsqrt(ms + EPS) * nw).astype(o_ref.dtype)

    xbuf_ref[0:HALO, :] = xbuf_ref[r:r + HALO, :]
    bbuf_ref[0:HALO, :] = bbuf_ref[r:r + HALO, :]
    cbuf_ref[0:HALO, :] = cbuf_ref[r:r + HALO, :]


def _ssd_branch(proj, dt_raw, conv_w, conv_b, dt_bias, a_log, d, norm_w, inner, rows_per_step=1024):
    s = proj.shape[0]
    h = dt_bias.shape[0]
    g = SSD_GROUPS
    hpg = h // g
    gw = hpg * SSD_HEAD_DIM
    n = SSD_STATE
    r = _pick_tile(s, rows_per_step)
    kc = conv_w.shape[0]
    assert r % CHUNK == 0 and inner == g * gw and gw % LANES == 0 and h <= LANES and kc - 1 <= HALO
    a = -jnp.exp(a_log)
    dt_t = dt_raw[:, :h].T.reshape(g, hpg, s)
    bias = jnp.pad(dt_bias, (0, LANES - h)).reshape(1, LANES)
    bias_t = jnp.broadcast_to(dt_bias.reshape(g, hpg, 1), (g, hpg, CHUNK))
    arow = jnp.pad(a, (0, LANES - h)).reshape(1, LANES)
    dexp = jnp.repeat(d, SSD_HEAD_DIM).reshape(g, 1, gw)
    acol = jnp.broadcast_to(a.reshape(g, hpg, 1), (g, hpg, CHUNK))
    lane_head = jnp.arange(inner) // SSD_HEAD_DIM
    e = (jnp.arange(LANES)[:, None] == lane_head[None, :]).astype(BF16)
    e = e.reshape(LANES, g, gw).transpose(1, 0, 2)
    conv_b = conv_b.reshape(1, -1)
    xb0 = inner // gw
    bb0 = 2 * inner // n
    cb0 = bb0 + g
    wb0 = inner // n
    wc0 = wb0 + g
    per_g = lambda gi, t: (gi, 0, 0)
    return pl.pallas_call(
        functools.partial(_ssd_kernel, nchunk=r // CHUNK, hpg=hpg),
        grid=(g, s // r),
        in_specs=[pl.BlockSpec((r, gw), lambda gi, t: (t, xb0 + gi)),
                  pl.BlockSpec((r, n), lambda gi, t: (t, bb0 + gi)),
                  pl.BlockSpec((r, n), lambda gi, t: (t, cb0 + gi)),
                  pl.BlockSpec((r, gw), lambda gi, t: (t, gi)),
                  pl.BlockSpec((r, LANES), lambda gi, t: (t, 0)),
                  pl.BlockSpec((None, hpg, r), lambda gi, t: (gi, 0, t)),
                  pl.BlockSpec((1, LANES), lambda gi, t: (0, 0)),
                  pl.BlockSpec((None, hpg, CHUNK), per_g),
                  pl.BlockSpec((kc, gw), lambda gi, t: (0, gi)),
                  pl.BlockSpec((kc, n), lambda gi, t: (0, wb0 + gi)),
                  pl.BlockSpec((kc, n), lambda gi, t: (0, wc0 + gi)),
                  pl.BlockSpec((1, gw), lambda gi, t: (0, gi)),
                  pl.BlockSpec((1, n), lambda gi, t: (0, wb0 + gi)),
                  pl.BlockSpec((1, n), lambda gi, t: (0, wc0 + gi)),
                  pl.BlockSpec((1, LANES), lambda gi, t: (0, 0)),
                  pl.BlockSpec((None, hpg, CHUNK), per_g),
                  pl.BlockSpec((None, LANES, gw), per_g),
                  pl.BlockSpec((None, 1, gw), per_g),
                  pl.BlockSpec((1, gw), lambda gi, t: (0, gi))],
        out_specs=pl.BlockSpec((r, gw), lambda gi, t: (t, gi)),
        out_shape=jax.ShapeDtypeStruct((s, inner), BF16),
        scratch_shapes=[pltpu.VMEM((n, gw), F32), pltpu.VMEM((HALO + r, gw), F32),
                        pltpu.VMEM((HALO + r, n), F32), pltpu.VMEM((HALO + r, n), F32)],
        compiler_params=_compiler_params(("parallel", "arbitrary")),
        name="ssd_branch",
    )(proj, proj, proj, proj, dt_raw, dt_t, bias, bias_t, conv_w, conv_w, conv_w,
      conv_b, conv_b, conv_b, arow, acol, e, dexp, norm_w.reshape(1, inner))


S5_BLOCK = 8


def _cmul(ar, ai, br, bi):
    return ar * br - ai * bi, ar * bi + ai * br


def _log2(n):
    assert n > 0 and n & (n - 1) == 0
    return n.bit_length() - 1


def _s5_expand_weights(ab_ref, ac_ref, kc_ref, wb_ref, wc_ref, t_ref):
    rcb, cn = ab_ref.shape
    w2 = wb_ref.shape[1]
    cb = t_ref.shape[1]
    n5 = cn // 2
    gpb = w2 // cn
    ch = cb // gpb
    step = MXU_DIM
    iota = lambda shape, dim: lax.broadcasted_iota(jnp.int32, shape, dim)

    def group(idx, unit):
        return (idx >> _log2(unit)) & (gpb - 1)

    for c0 in range(0, w2, step):
        r = iota((cn, step), 0)
        j = iota((cn, step), 1) + c0
        onehot = ((r >> _log2(n5)) == (j >> _log2(gpb * n5))) & ((r & (n5 - 1)) == (j & (n5 - 1)))
        spread = _dot(ab_ref[...], onehot.astype(BF16))
        keep = group(iota((rcb, step), 0), ch) == group(iota((rcb, step), 1) + c0, n5)
        wb_ref[:, c0:c0 + step] = jnp.where(keep, spread, 0.0).astype(BF16)
    for r0 in range(0, w2, step):
        i = iota((step, cn), 0) + r0
        r = iota((step, cn), 1)
        onehot = ((r >> _log2(n5)) == (i >> _log2(gpb * n5))) & ((r & (n5 - 1)) == (i & (n5 - 1)))
        spread = _dot(onehot.astype(BF16), ac_ref[...])
        keep = group(iota((step, rcb), 0) + r0, n5) == group(iota((step, rcb), 1), ch)
        wc_ref[r0:r0 + step, :] = jnp.where(keep, spread, 0.0).astype(BF16)
    r = iota((kc_ref.shape[1], cb), 0)
    j = iota((kc_ref.shape[1], cb), 1)
    spread = _dot(kc_ref[...], (r == (j & (ch - 1))).astype(BF16))
    keep = group(iota((rcb, cb), 0), ch) == group(iota((rcb, cb), 1), ch)
    t_ref[...] = jnp.where(keep, spread, 0.0).astype(BF16)


def _s5_kernel(u_ref, ab_ref, ac_ref, kc_ref, k_ref, d_ref, o_ref,
               wb_ref, wc_ref, t_ref, h_ref, carry_ref):
    m, rcb = u_ref.shape
    cb = t_ref.shape[1]
    nr = rcb // cb
    w = h_ref.shape[1] // 2
    ns = SUBLANES

    @pl.when(pl.program_id(1) == 0)
    def _():
        carry_ref[...] = jnp.zeros_like(carry_ref)
        _s5_expand_weights(ab_ref, ac_ref, kc_ref, wb_ref, wc_ref, t_ref)

    h_ref[...] = _dot(u_ref[...], wb_ref[...])
    first_row = lax.broadcasted_iota(jnp.int32, (ns, w), 0) == 0

    def scan(r, carry):
        cr, ci = carry
        rows = pl.ds(pl.multiple_of(r * ns, ns), ns)
        xr = h_ref[rows, :w]
        xi = h_ref[rows, w:]
        for idx, shift in enumerate((1, 2, 4)):
            pr, pi = _cmul(k_ref[2 * idx], k_ref[2 * idx + 1],
                           pltpu.roll(xr, shift, 0), pltpu.roll(xi, shift, 0))
            xr, xi = xr + pr, xi + pi
        pr, pi = _cmul(k_ref[6], k_ref[7], cr, ci)
        xr, xi = xr + pr, xi + pi
        h_ref[rows, :w] = jnp.where(first_row, cr, pltpu.roll(xr, 1, 0))
        h_ref[rows, w:] = jnp.where(first_row, ci, pltpu.roll(xi, 1, 0))
        return xr[ns - 1:ns, :], xi[ns - 1:ns, :]

    cr, ci = lax.fori_loop(0, m // ns, scan, (carry_ref[0:1, :w], carry_ref[0:1, w:]))
    carry_ref[0:1, :w] = cr
    carry_ref[0:1, w:] = ci

    y_state = _dot(h_ref[...].astype(BF16), wc_ref[...])
    for qo in range(nr):
        lo = slice(qo * cb, (qo + 1) * cb)
        acc = y_state[:, lo] + d_ref[:, lo] * u_ref[:, lo].astype(F32)
        for qi in range(qo + 1):
            acc = acc + _dot(u_ref[:, qi * cb:(qi + 1) * cb], t_ref[(qo - qi) * cb:(qo - qi + 1) * cb, :])
        o_ref[:, lo] = jax.nn.gelu(acc).astype(o_ref.dtype)


def _s5_constants(lam_re, lam_im, log_dt, b_re, b_im, c_re, c_im):
    hi = lax.Precision.HIGHEST
    nl, g5, n5 = lam_re.shape
    ch = S5_GROUP_CH
    nr = S5_BLOCK
    gpb = min(S5_GROUPS_PER_BLOCK, g5)
    nblk = g5 // gpb
    dt = jnp.exp(log_dt)[..., None]
    mag = jnp.exp(lam_re * dt)
    ang = lam_im * dt
    abar_re, abar_im = mag * jnp.cos(ang), mag * jnp.sin(ang)
    den = lam_re * lam_re + lam_im * lam_im
    nre, nim = abar_re - 1.0, abar_im
    coef_re = (nre * lam_re + nim * lam_im) / den
    coef_im = (nim * lam_re - nre * lam_im) / den
    bbar_re = coef_re[..., None] * b_re - coef_im[..., None] * b_im
    bbar_im = coef_re[..., None] * b_im + coef_im[..., None] * b_re

    def powers(pr, pi, count):
        out = [(jnp.ones_like(pr), jnp.zeros_like(pi))]
        for _ in range(count):
            out.append(_cmul(out[-1][0], out[-1][1], pr, pi))
        return jnp.stack([o[0] for o in out], axis=1), jnp.stack([o[1] for o in out], axis=1)

    ap_re, ap_im = powers(abar_re, abar_im, nr)
    bp_re, bp_im = powers(ap_re[:, nr], ap_im[:, nr], SUBLANES)

    ab_re, ab_im = _cmul(ap_re[:, :nr, :, :, None], ap_im[:, :nr, :, :, None],
                         bbar_re[:, None], bbar_im[:, None])
    taps = (jnp.einsum('lgon,ldgni->ldgoi', c_re, ab_re, precision=hi)
            - jnp.einsum('lgon,ldgni->ldgoi', c_im, ab_im, precision=hi))
    kc = taps.reshape(nl, nr, nblk, gpb, ch, ch).transpose(0, 2, 1, 3, 5, 4)
    kc = jnp.pad(kc.reshape(nl, nblk, nr * gpb * ch, ch), ((0, 0), (0, 0), (0, 0), (0, LANES - ch)))

    ab = jnp.stack([ab_re[:, ::-1], ab_im[:, ::-1]], axis=2)
    ab = ab.reshape(nl, nr, 2, nblk, gpb, n5, ch).transpose(0, 3, 1, 4, 6, 2, 5)
    ab = ab.reshape(nl, nblk, nr * gpb * ch, 2 * n5)

    ca_re, ca_im = _cmul(c_re[:, None], c_im[:, None],
                         ap_re[:, 1:, :, None, :], ap_im[:, 1:, :, None, :])
    ac = jnp.stack([ca_re, -ca_im], axis=2)
    ac = ac.reshape(nl, nr, 2, nblk, gpb, ch, n5).transpose(0, 3, 2, 6, 1, 4, 5)
    ac = ac.reshape(nl, nblk, 2 * n5, nr * gpb * ch)

    def lanes(v):
        return v.reshape(nl, v.shape[1], nblk, gpb * n5).transpose(0, 2, 1, 3)

    rows = jnp.arange(SUBLANES)[:, None]
    kinds = []
    for shift in (1, 2, 4):
        for part in (bp_re, bp_im):
            kinds.append(jnp.where(rows >= shift, lanes(part[:, shift:shift + 1]), 0.0))
    for part in (bp_re, bp_im):
        kinds.append(lanes(part[:, 1:]))
    kconst = jnp.stack(kinds, axis=2)
    return ab.astype(BF16), ac.astype(BF16), kc.astype(BF16), kconst


def _s5_branch(u, consts, layer, d, blocks_per_step=512):
    nblk, s, cb = u.shape
    nr = S5_BLOCK
    ab, ac, kc, kconst = consts
    w2 = 2 * kconst.shape[4]
    assert s % nr == 0 and ab.shape[1:3] == (nblk, nr * cb)
    m = _pick_tile(s // nr, blocks_per_step)
    u2 = u.reshape(nblk, s // nr, nr * cb)
    d2 = jnp.tile(d.reshape(nblk, 1, cb), (1, 1, nr))
    out = pl.pallas_call(
        _s5_kernel,
        grid=(nblk, s // nr // m),
        in_specs=[pl.BlockSpec((None, m, nr * cb), lambda gb, t: (gb, t, 0)),
                  pl.BlockSpec((None, None) + ab.shape[2:], lambda gb, t: (layer, gb, 0, 0)),
                  pl.BlockSpec((None, None) + ac.shape[2:], lambda gb, t: (layer, gb, 0, 0)),
                  pl.BlockSpec((None, None) + kc.shape[2:], lambda gb, t: (layer, gb, 0, 0)),
                  pl.BlockSpec((None, None, 8, SUBLANES, w2 // 2), lambda gb, t: (layer, gb, 0, 0, 0)),
                  pl.BlockSpec((None, 1, nr * cb), lambda gb, t: (gb, 0, 0))],
        out_specs=pl.BlockSpec((None, m, nr * cb), lambda gb, t: (gb, t, 0)),
        out_shape=jax.ShapeDtypeStruct((nblk, s // nr, nr * cb), BF16),
        scratch_shapes=[pltpu.VMEM((nr * cb, w2), BF16), pltpu.VMEM((w2, nr * cb), BF16),
                        pltpu.VMEM((nr * cb, cb), BF16),
                        pltpu.VMEM((m, w2), F32), pltpu.VMEM((SUBLANES, w2), F32)],
        compiler_params=_compiler_params(("parallel", "arbitrary")),
        name="s5_branch",
    )(u2, ab, ac, kc, kconst, d2)
    return out.reshape(nblk, s, cb)


def kernel(x, c, ada_w1, ada_w2, ada_b, ada_table, norm_mix_w, w_in, ssd_conv_w, ssd_conv_b,
           ssd_dt_bias, ssd_a_log, ssd_d, ssd_norm_w, w_ssd_out, s5_lam_re, s5_lam_im, s5_log_dt,
           s5_b_re, s5_b_im, s5_c_re, s5_c_im, s5_d, w_glu, w_out, norm_ffn_w, w_up, ffn_conv_w,
           ffn_conv_b, w_down, final_norm_w):
    bsz, seq, dm = x.shape
    assert bsz == 1
    depth = w_in.shape[0]
    heads = ssd_dt_bias.shape[1]
    inner = heads * SSD_HEAD_DIM
    xbc_w = inner + 2 * SSD_GROUPS * SSD_STATE
    w5 = s5_d.shape[1]
    o_dt = inner + xbc_w
    o_u5 = o_dt + heads
    s5_cb = min(S5_GROUPS_PER_BLOCK, w5 // S5_GROUP_CH) * S5_GROUP_CH

    cpad = jnp.pad(jax.nn.silu(c), ((0, 2 * SUBLANES - bsz), (0, 0))).astype(BF16)
    t1 = _matmul(cpad, ada_w1.astype(BF16), BF16)
    mod = (_matmul(t1, ada_w2.astype(BF16), F32)[0] + ada_b).reshape(N_MOD, dm)

    s5_consts = _s5_constants(s5_lam_re, s5_lam_im, s5_log_dt, s5_b_re, s5_b_im, s5_c_re, s5_c_im)
    w_in_b, w_ssd_out_b, w_glu_b, w_out_b, w_up_b, w_down_b = (
        w.astype(BF16) for w in (w_in, w_ssd_out, w_glu, w_out, w_up, w_down))

    xs = x[0]
    for l in range(depth):
        m = mod + ada_table[l]
        h = _norm_mod(xs, norm_mix_w[l] * (1.0 + m[1]), m[0], BF16)
        w_dt = jnp.pad(w_in_b[l, :, o_dt:o_u5], ((0, 0), (0, LANES - heads)))
        proj = _matmul(h, w_in_b, BF16, layer=l, n=o_dt)
        gate_logits = _matmul(h, w_in_b[l, :, o_u5 + w5:], BF16)
        u5 = _matmul_slabs(h, w_in_b[l, :, o_u5:o_u5 + w5], s5_cb, BF16)
        dt_raw = _matmul(h, w_dt, F32)
        g_ssd = _ssd_branch(proj, dt_raw, ssd_conv_w[l], ssd_conv_b[l], ssd_dt_bias[l],
                            ssd_a_log[l], ssd_d[l], ssd_norm_w[l], inner)
        g_s5 = _s5_branch(u5, s5_consts, l, s5_d[l])
        y_s5 = _matmul_glu(g_s5, w_glu_b, l, BF16)
        mix = _matmul_mix(g_ssd, w_ssd_out_b, l, gate_logits, 0, y_s5, BF16)
        xs = _matmul_resid(mix, w_out_b, l, xs, m[2])
        h = _norm_mod(xs, norm_ffn_w[l] * (1.0 + m[4]), m[3], BF16)
        act = _matmul_convgate(h, w_up_b, l, ffn_conv_w[l], ffn_conv_b[l], BF16)
        xs = _matmul_resid(act, w_down_b, l, xs, m[5])
    out = _norm_mod(xs, final_norm_w, jnp.zeros_like(final_norm_w), F32)
    return out[None]
```

```python
import functools
import math

import jax
import jax.numpy as jnp
from jax import lax
from jax.experimental import pallas as pl
from jax.experimental.pallas import tpu as pltpu

F32 = jnp.float32
BF16 = jnp.bfloat16

SSD_HEAD_DIM = 64
SSD_GROUPS = 8
SSD_STATE = 128
CHUNK = 128
S5_GROUP_CH = 16
N_MOD = 6
EPS = 1e-6

V7X_VMEM_BYTES = 64 * 1024 * 1024
VMEM_LIMIT_BYTES = V7X_VMEM_BYTES - 8 * 1024 * 1024
SUBLANES = 8
LANES = 128
MXU_DIM = 256
S5_GROUPS_PER_BLOCK = MXU_DIM // S5_GROUP_CH
HALO = SUBLANES


def _compiler_params(semantics):
    return pltpu.CompilerParams(dimension_semantics=semantics,
                                vmem_limit_bytes=VMEM_LIMIT_BYTES)


def _pick_tile(dim, want):
    t = min(dim, want)
    while dim % t:
        t //= 2
    return t


def _dot(a, b):
    return jnp.dot(a, b, preferred_element_type=F32)


def _weight_spec(w, layer, block, index_map):
    if layer is None:
        assert w.ndim == 2
        return pl.BlockSpec(block, index_map)
    assert w.ndim == 3
    return pl.BlockSpec((None,) + block, lambda *idx: (layer,) + index_map(*idx))


def _sigmoid(x):
    return jax.nn.sigmoid(x)


def _mm_kernel(a_ref, b_ref, o_ref, *scratch, nk, gate=False):
    if gate:
        a = a_ref[...]
        for sl in _col_slices(o_ref.shape[1]):
            o_ref[:, sl] = _sigmoid(_dot(a, b_ref[:, sl])).astype(o_ref.dtype)
        return
    if nk == 1:
        o_ref[...] = _dot(a_ref[...], b_ref[...]).astype(o_ref.dtype)
        return
    acc_ref, = scratch
    k = pl.program_id(2)

    @pl.when(k == 0)
    def _():
        acc_ref[...] = jnp.zeros_like(acc_ref)

    acc_ref[...] += _dot(a_ref[...], b_ref[...])

    @pl.when(k == nk - 1)
    def _():
        o_ref[...] = acc_ref[...].astype(o_ref.dtype)


def _k_tiling(k, tk):
    if k % tk:
        tk = k // 2 if k > tk else k
    tk = min(tk, k)
    assert k % tk == 0
    return tk, k // tk


def _matmul(a, b, out_dtype, tm=1024, tn=1024, tk=4096, layer=None, n=None, gate=False):
    m, k = a.shape
    n = b.shape[-1] if n is None else n
    tm, tn = _pick_tile(m, tm), _pick_tile(n, tn)
    tk, nk = _k_tiling(k, tk)
    assert nk == 1 or not gate
    scratch = [pltpu.VMEM((tm, tn), F32)] if nk > 1 else []
    return pl.pallas_call(
        functools.partial(_mm_kernel, nk=nk, gate=gate),
        grid=(m // tm, n // tn, nk),
        in_specs=[pl.BlockSpec((tm, tk), lambda i, j, kk: (i, kk)),
                  _weight_spec(b, layer, (tk, tn), lambda i, j, kk: (kk, j))],
        out_specs=pl.BlockSpec((tm, tn), lambda i, j, kk: (i, j)),
        out_shape=jax.ShapeDtypeStruct((m, n), out_dtype),
        scratch_shapes=scratch,
        compiler_params=_compiler_params(("parallel", "parallel", "arbitrary")),
        name="matmul",
    )(a, b)


def _col_slices(tn):
    cs = min(tn, MXU_DIM)
    return [slice(c, c + cs) for c in range(0, tn, cs)]


def _mm_slab_kernel(a_ref, b_ref, o_ref):
    acc = _dot(a_ref[...], b_ref[...])
    cb = o_ref.shape[2]
    for sidx in range(o_ref.shape[0]):
        o_ref[sidx] = acc[:, sidx * cb:(sidx + 1) * cb].astype(o_ref.dtype)


def _matmul_slabs(a, w, cb, out_dtype, tm=1024, tn=1024):
    m, k = a.shape
    n = w.shape[1]
    tm, tn = _pick_tile(m, tm), _pick_tile(n, tn)
    assert tn % cb == 0
    spb = tn // cb
    return pl.pallas_call(
        _mm_slab_kernel,
        grid=(m // tm, n // tn),
        in_specs=[pl.BlockSpec((tm, k), lambda i, j: (i, 0)),
                  pl.BlockSpec((k, tn), lambda i, j: (0, j))],
        out_specs=pl.BlockSpec((spb, tm, cb), lambda i, j: (j, i, 0)),
        out_shape=jax.ShapeDtypeStruct((n // cb, m, cb), out_dtype),
        compiler_params=_compiler_params(("parallel", "parallel")),
        name="matmul_slabs",
    )(a, w)


def _mm_glu_kernel(*refs, nslab):
    a_refs, (bv_ref, bg_ref, o_ref) = refs[:nslab], refs[nslab:]
    a = jnp.concatenate([r[...] for r in a_refs], axis=1)
    for sl in _col_slices(o_ref.shape[1]):
        v = _dot(a, bv_ref[:, sl])
        g = _dot(a, bg_ref[:, sl])
        o_ref[:, sl] = (v * _sigmoid(g)).astype(o_ref.dtype)


def _matmul_glu(a_slabs, w, layer, out_dtype, tm=1024, tn=1024):
    nslab, m, cb = a_slabs.shape
    k = nslab * cb
    n = w.shape[-1] // 2
    tm, tn = _pick_tile(m, tm), _pick_tile(n, tn)
    nj = n // tn
    slab_specs = [pl.BlockSpec((None, tm, cb), functools.partial(lambda i, j, sidx: (sidx, i, 0), sidx=sidx))
                  for sidx in range(nslab)]
    return pl.pallas_call(
        functools.partial(_mm_glu_kernel, nslab=nslab),
        grid=(m // tm, nj),
        in_specs=slab_specs + [_weight_spec(w, layer, (k, tn), lambda i, j: (0, j)),
                               _weight_spec(w, layer, (k, tn), lambda i, j: (0, j + nj))],
        out_specs=pl.BlockSpec((tm, tn), lambda i, j: (i, j)),
        out_shape=jax.ShapeDtypeStruct((m, n), out_dtype),
        compiler_params=_compiler_params(("parallel", "parallel")),
        name="matmul_glu",
    )(*([a_slabs] * nslab), w, w)


def _mm_mix_kernel(a_ref, b_ref, g1_ref, g2_ref, y2_ref, o_ref):
    a = a_ref[...]
    for sl in _col_slices(o_ref.shape[1]):
        y1 = _dot(a, b_ref[:, sl])
        o_ref[:, sl] = (g1_ref[:, sl].astype(F32) * y1
                        + g2_ref[:, sl].astype(F32) * y2_ref[:, sl].astype(F32)
                        ).astype(o_ref.dtype)


def _matmul_mix(a, w, layer, proj, gate_col, y2, out_dtype, tm=1024):
    m, k = a.shape
    n = w.shape[-1]
    tm = _pick_tile(m, tm)
    tn = math.gcd(math.gcd(512, gate_col), n)
    nj = n // tn
    g0 = gate_col // tn
    return pl.pallas_call(
        _mm_mix_kernel,
        grid=(m // tm, nj),
        in_specs=[pl.BlockSpec((tm, k), lambda i, j: (i, 0)),
                  _weight_spec(w, layer, (k, tn), lambda i, j: (0, j)),
                  pl.BlockSpec((tm, tn), lambda i, j: (i, g0 + j)),
                  pl.BlockSpec((tm, tn), lambda i, j: (i, g0 + nj + j)),
                  pl.BlockSpec((tm, tn), lambda i, j: (i, j))],
        out_specs=pl.BlockSpec((tm, tn), lambda i, j: (i, j)),
        out_shape=jax.ShapeDtypeStruct((m, n), out_dtype),
        compiler_params=_compiler_params(("parallel", "parallel")),
        name="matmul_mix",
    )(a, w, proj, proj, y2)


def _mm_resid_kernel(a_ref, b_ref, x_ref, g_ref, o_ref, *scratch, nk):
    if nk == 1:
        a = a_ref[...]
        for sl in _col_slices(o_ref.shape[1]):
            o_ref[:, sl] = x_ref[:, sl] + g_ref[:, sl] * _dot(a, b_ref[:, sl])
        return
    acc_ref, = scratch
    k = pl.program_id(2)

    @pl.when(k == 0)
    def _():
        acc_ref[...] = jnp.zeros_like(acc_ref)

    acc_ref[...] += _dot(a_ref[...], b_ref[...])

    @pl.when(k == nk - 1)
    def _():
        o_ref[...] = x_ref[...] + g_ref[...] * acc_ref[...]


def _matmul_resid(a, w, layer, x, gate, tm=1024, tk=8192):
    m, k = a.shape
    n = w.shape[-1]
    tk, nk = _k_tiling(k, tk)
    tm, tn = _pick_tile(m, tm), _pick_tile(n, 1024 if nk > 1 else 512)
    scratch = [pltpu.VMEM((tm, tn), F32)] if nk > 1 else []
    return pl.pallas_call(
        functools.partial(_mm_resid_kernel, nk=nk),
        grid=(m // tm, n // tn, nk),
        in_specs=[pl.BlockSpec((tm, tk), lambda i, j, kk: (i, kk)),
                  _weight_spec(w, layer, (tk, tn), lambda i, j, kk: (kk, j)),
                  pl.BlockSpec((tm, tn), lambda i, j, kk: (i, j)),
                  pl.BlockSpec((1, tn), lambda i, j, kk: (0, j))],
        out_specs=pl.BlockSpec((tm, tn), lambda i, j, kk: (i, j)),
        out_shape=jax.ShapeDtypeStruct((m, n), F32),
        scratch_shapes=scratch,
        compiler_params=_compiler_params(("parallel", "parallel", "arbitrary")),
        name="matmul_resid",
    )(a, w, x, gate.reshape(1, n))


def _conv_taps(buf_ref, w_ref, b_ref, base, rows, sl):
    k = w_ref.shape[0]
    out = b_ref[:, sl] + w_ref[k - 1:k, sl] * buf_ref[pl.ds(base, rows), sl]
    for j in range(1, k):
        out = out + w_ref[k - 1 - j:k - j, sl] * buf_ref[pl.ds(base - j, rows), sl]
    return out


def _mm_convgate_kernel(a_ref, ba_ref, bv_ref, wa_ref, wv_ref, ca_ref, cv_ref, o_ref,
                        ua_ref, uv_ref, halo_ref):
    i = pl.program_id(0)
    j = pl.program_id(1)
    tm = a_ref.shape[0]

    @pl.when(i == 0)
    def _():
        halo_ref[j] = jnp.zeros(halo_ref.shape[1:], F32)

    ua_ref[0:HALO, :] = halo_ref[j, 0]
    uv_ref[0:HALO, :] = halo_ref[j, 1]
    a = a_ref[...]
    for sl in _col_slices(o_ref.shape[1]):
        ua_ref[HALO:, sl] = _dot(a, ba_ref[:, sl])
        uv_ref[HALO:, sl] = _dot(a, bv_ref[:, sl])
        ca = _conv_taps(ua_ref, wa_ref, ca_ref, HALO, tm, sl)
        cv = _conv_taps(uv_ref, wv_ref, cv_ref, HALO, tm, sl)
        o_ref[:, sl] = (ca * _sigmoid(ca) * cv).astype(o_ref.dtype)
    halo_ref[j, 0] = ua_ref[tm:tm + HALO, :]
    halo_ref[j, 1] = uv_ref[tm:tm + HALO, :]


def _matmul_convgate(a, w, layer, conv_w, conv_b, out_dtype, tm=1024):
    m, k = a.shape
    n = w.shape[-1] // 2
    kc = conv_w.shape[0]
    assert kc - 1 <= HALO
    tm = _pick_tile(m, tm)
    tn = math.gcd(512, n)
    nj = n // tn
    conv_b = conv_b.reshape(1, 2 * n)
    return pl.pallas_call(
        _mm_convgate_kernel,
        grid=(m // tm, nj),
        in_specs=[pl.BlockSpec((tm, k), lambda i, j: (i, 0)),
                  _weight_spec(w, layer, (k, tn), lambda i, j: (0, j)),
                  _weight_spec(w, layer, (k, tn), lambda i, j: (0, j + nj)),
                  pl.BlockSpec((kc, tn), lambda i, j: (0, j)),
                  pl.BlockSpec((kc, tn), lambda i, j: (0, j + nj)),
                  pl.BlockSpec((1, tn), lambda i, j: (0, j)),
                  pl.BlockSpec((1, tn), lambda i, j: (0, j + nj))],
        out_specs=pl.BlockSpec((tm, tn), lambda i, j: (i, j)),
        out_shape=jax.ShapeDtypeStruct((m, n), out_dtype),
        scratch_shapes=[pltpu.VMEM((tm + HALO, tn), F32), pltpu.VMEM((tm + HALO, tn), F32),
                        pltpu.VMEM((nj, 2, HALO, tn), F32)],
        compiler_params=_compiler_params(("arbitrary", "arbitrary")),
        name="matmul_convgate",
    )(a, w, w, conv_w, conv_w, conv_b, conv_b)


def _norm_kernel(x_ref, w_ref, s_ref, o_ref):
    x = x_ref[...]
    ms = jnp.mean(x * x, axis=-1, keepdims=True)
    o_ref[...] = (x * lax.rsqrt(ms + EPS) * w_ref[...] + s_ref[...]).astype(o_ref.dtype)


def _norm_mod(x, w, shift, out_dtype, tr=512):
    s, d = x.shape
    tr = _pick_tile(s, tr)
    return pl.pallas_call(
        _norm_kernel,
        grid=(s // tr,),
        in_specs=[pl.BlockSpec((tr, d), lambda i: (i, 0)),
                  pl.BlockSpec((1, d), lambda i: (0, 0)),
                  pl.BlockSpec((1, d), lambda i: (0, 0))],
        out_specs=pl.BlockSpec((tr, d), lambda i: (i, 0)),
        out_shape=jax.ShapeDtypeStruct((s, d), out_dtype),
        compiler_params=_compiler_params(("parallel",)),
        name="norm_mod",
    )(x, w.reshape(1, d), shift.reshape(1, d))


def _split3(x):
    hi = x.astype(BF16)
    r1 = x - hi.astype(F32)
    mid = r1.astype(BF16)
    lo = (r1 - mid.astype(F32)).astype(BF16)
    return hi, mid, lo


def _dot01_left(t01, x):
    hi, mid, lo = _split3(x)
    return _dot(t01, hi) + _dot(t01, mid) + _dot(t01, lo)


def _dot01_right(x, t01):
    hi, mid, lo = _split3(x)
    return _dot(hi, t01) + _dot(mid, t01) + _dot(lo, t01)


def _softplus(x):
    return jnp.maximum(x, 0.0) + jnp.log1p(jnp.exp(-jnp.abs(x)))


def _ssd_kernel(x_ref, b_ref, c_ref, z_ref, dt_ref, dtT_ref, bias_ref, biasT_ref,
                wx_ref, wb_ref, wc_ref, cbx_ref, cbb_ref, cbc_ref,
                arow_ref, acol_ref, e_ref, dexp_ref, nw_ref,
                o_ref, state_ref, xbuf_ref, bbuf_ref, cbuf_ref, *, nchunk, hpg):
    q = CHUNK
    p = SSD_HEAD_DIM
    r = x_ref.shape[0]

    @pl.when(pl.program_id(1) == 0)
    def _():
        state_ref[...] = jnp.zeros_like(state_ref)
        xbuf_ref[0:HALO, :] = jnp.zeros((HALO, xbuf_ref.shape[1]), F32)
        bbuf_ref[0:HALO, :] = jnp.zeros((HALO, bbuf_ref.shape[1]), F32)
        cbuf_ref[0:HALO, :] = jnp.zeros((HALO, cbuf_ref.shape[1]), F32)

    xbuf_ref[HALO:, :] = x_ref[...].astype(F32)
    bbuf_ref[HALO:, :] = b_ref[...].astype(F32)
    cbuf_ref[HALO:, :] = c_ref[...].astype(F32)

    row = lax.broadcasted_iota(jnp.int32, (q, q), 0)
    col = lax.broadcasted_iota(jnp.int32, (q, q), 1)
    mask = row >= col
    tri = mask.astype(BF16)
    tri_t = (row <= col).astype(BF16)
    e = e_ref[...]
    dexp = dexp_ref[...]
    acol = acol_ref[...]
    nw = nw_ref[...]
    full = slice(None)

    def conv_silu(buf_ref, w_ref, cb_ref, base):
        v = _conv_taps(buf_ref, w_ref, cb_ref, base, q, full)
        return v * _sigmoid(v)

    for ci in range(nchunk):
        sl = pl.ds(ci * q, q)
        base = HALO + ci * q
        x = conv_silu(xbuf_ref, wx_ref, cbx_ref, base)
        b = conv_silu(bbuf_ref, wb_ref, cbb_ref, base).astype(BF16)
        c = conv_silu(cbuf_ref, wc_ref, cbc_ref, base).astype(BF16)
        dt = _softplus(dt_ref[sl, :] + bias_ref[...])
        dt_t = _softplus(dtT_ref[:, sl] + biasT_ref[...])
        dt_exp = _dot01_right(dt, e)
        cum_h = _dot01_left(tri, dt * arow_ref[...])
        cum = _dot01_right(cum_h, e)
        cum_t = _dot01_right(dt_t * acol, tri_t)
        cum_last = cum[q - 1:q, :]
        xdt = x * dt_exp
        cb = lax.dot_general(c, b, (((1,), (1,)), ((), ())), preferred_element_type=F32)
        st = state_ref[...]
        y_inter = _dot(c, st.astype(BF16)) * jnp.exp(cum)
        ys = []
        for h in range(hpg):
            seg = cum[:, h * p:h * p + 1] - cum_t[h:h + 1, :]
            dec = jnp.exp(jnp.where(mask, seg, -jnp.inf))
            w = (cb * dec).astype(BF16)
            ys.append(_dot(w, xdt[:, h * p:(h + 1) * p].astype(BF16)))
        y = jnp.concatenate(ys, axis=1) + y_inter + dexp * x
        dec_end = jnp.exp(cum_last - cum)
        b_t = b.astype(F32).T.astype(BF16)
        state_ref[...] = st * jnp.exp(cum_last) + _dot(b_t, (xdt * dec_end).astype(BF16))
        zf = z_ref[sl, :].astype(F32)
        gt = y * (zf * _sigmoid(zf))
        ms = jnp.mean(gt * gt, axis=-1, keepdims=True)
        o_ref[sl, :] = (gt * lax.rsqrt(ms + EPS) * nw).astype(o_ref.dtype)

    xbuf_ref[0:HALO, :] = xbuf_ref[r:r + HALO, :]
    bbuf_ref[0:HALO, :] = bbuf_ref[r:r + HALO, :]
    cbuf_ref[0:HALO, :] = cbuf_ref[r:r + HALO, :]


def _ssd_branch(proj, dt_raw, conv_w, conv_b, dt_bias, a_log, d, norm_w, inner, rows_per_step=1024):
    s = proj.shape[0]
    h = dt_bias.shape[0]
    g = SSD_GROUPS
    hpg = h // g
    gw = hpg * SSD_HEAD_DIM
    n = SSD_STATE
    r = _pick_tile(s, rows_per_step)
    kc = conv_w.shape[0]
    assert r % CHUNK == 0 and inner == g * gw and gw % LANES == 0 and h <= LANES and kc - 1 <= HALO
    a = -jnp.exp(a_log)
    dt_t = dt_raw[:, :h].T.reshape(g, hpg, s)
    bias = jnp.pad(dt_bias, (0, LANES - h)).reshape(1, LANES)
    bias_t = jnp.broadcast_to(dt_bias.reshape(g, hpg, 1), (g, hpg, CHUNK))
    arow = jnp.pad(a, (0, LANES - h)).reshape(1, LANES)
    dexp = jnp.repeat(d, SSD_HEAD_DIM).reshape(g, 1, gw)
    acol = jnp.broadcast_to(a.reshape(g, hpg, 1), (g, hpg, CHUNK))
    lane_head = jnp.arange(inner) // SSD_HEAD_DIM
    e = (jnp.arange(LANES)[:, None] == lane_head[None, :]).astype(BF16)
    e = e.reshape(LANES, g, gw).transpose(1, 0, 2)
    conv_b = conv_b.reshape(1, -1)
    xb0 = inner // gw
    bb0 = 2 * inner // n
    cb0 = bb0 + g
    wb0 = inner // n
    wc0 = wb0 + g
    per_g = lambda gi, t: (gi, 0, 0)
    return pl.pallas_call(
        functools.partial(_ssd_kernel, nchunk=r // CHUNK, hpg=hpg),
        grid=(g, s // r),
        in_specs=[pl.BlockSpec((r, gw), lambda gi, t: (t, xb0 + gi)),
                  pl.BlockSpec((r, n), lambda gi, t: (t, bb0 + gi)),
                  pl.BlockSpec((r, n), lambda gi, t: (t, cb0 + gi)),
                  pl.BlockSpec((r, gw), lambda gi, t: (t, gi)),
                  pl.BlockSpec((r, LANES), lambda gi, t: (t, 0)),
                  pl.BlockSpec((None, hpg, r), lambda gi, t: (gi, 0, t)),
                  pl.BlockSpec((1, LANES), lambda gi, t: (0, 0)),
                  pl.BlockSpec((None, hpg, CHUNK), per_g),
                  pl.BlockSpec((kc, gw), lambda gi, t: (0, gi)),
                  pl.BlockSpec((kc, n), lambda gi, t: (0, wb0 + gi)),
                  pl.BlockSpec((kc, n), lambda gi, t: (0, wc0 + gi)),
                  pl.BlockSpec((1, gw), lambda gi, t: (0, gi)),
                  pl.BlockSpec((1, n), lambda gi, t: (0, wb0 + gi)),
                  pl.BlockSpec((1, n), lambda gi, t: (0, wc0 + gi)),
                  pl.BlockSpec((1, LANES), lambda gi, t: (0, 0)),
                  pl.BlockSpec((None, hpg, CHUNK), per_g),
                  pl.BlockSpec((None, LANES, gw), per_g),
                  pl.BlockSpec((None, 1, gw), per_g),
                  pl.BlockSpec((1, gw), lambda gi, t: (0, gi))],
        out_specs=pl.BlockSpec((r, gw), lambda gi, t: (t, gi)),
        out_shape=jax.ShapeDtypeStruct((s, inner), BF16),
        scratch_shapes=[pltpu.VMEM((n, gw), F32), pltpu.VMEM((HALO + r, gw), F32),
                        pltpu.VMEM((HALO + r, n), F32), pltpu.VMEM((HALO + r, n), F32)],
        compiler_params=_compiler_params(("parallel", "arbitrary")),
        name="ssd_branch",
    )(proj, proj, proj, proj, dt_raw, dt_t, bias, bias_t, conv_w, conv_w, conv_w,
      conv_b, conv_b, conv_b, arow, acol, e, dexp, norm_w.reshape(1, inner))


S5_BLOCK = 8


def _cmul(ar, ai, br, bi):
    return ar * br - ai * bi, ar * bi + ai * br


def _log2(n):
    assert n > 0 and n & (n - 1) == 0
    return n.bit_length() - 1


def _s5_expand_weights(ab_ref, ac_ref, kc_ref, wb_ref, wc_ref, t_ref):
    rcb, cn = ab_ref.shape
    w2 = wb_ref.shape[1]
    cb = t_ref.shape[1]
    n5 = cn // 2
    gpb = w2 // cn
    ch = cb // gpb
    step = MXU_DIM
    iota = lambda shape, dim: lax.broadcasted_iota(jnp.int32, shape, dim)

    def group(idx, unit):
        return (idx >> _log2(unit)) & (gpb - 1)

    for c0 in range(0, w2, step):
        r = iota((cn, step), 0)
        j = iota((cn, step), 1) + c0
        onehot = ((r >> _log2(n5)) == (j >> _log2(gpb * n5))) & ((r & (n5 - 1)) == (j & (n5 - 1)))
        spread = _dot(ab_ref[...], onehot.astype(BF16))
        keep = group(iota((rcb, step), 0), ch) == group(iota((rcb, step), 1) + c0, n5)
        wb_ref[:, c0:c0 + step] = jnp.where(keep, spread, 0.0).astype(BF16)
    for r0 in range(0, w2, step):
        i = iota((step, cn), 0) + r0
        r = iota((step, cn), 1)
        onehot = ((r >> _log2(n5)) == (i >> _log2(gpb * n5))) & ((r & (n5 - 1)) == (i & (n5 - 1)))
        spread = _dot(onehot.astype(BF16), ac_ref[...])
        keep = group(iota((step, rcb), 0) + r0, n5) == group(iota((step, rcb), 1), ch)
        wc_ref[r0:r0 + step, :] = jnp.where(keep, spread, 0.0).astype(BF16)
    r = iota((kc_ref.shape[1], cb), 0)
    j = iota((kc_ref.shape[1], cb), 1)
    spread = _dot(kc_ref[...], (r == (j & (ch - 1))).astype(BF16))
    keep = group(iota((rcb, cb), 0), ch) == group(iota((rcb, cb), 1), ch)
    t_ref[...] = jnp.where(keep, spread, 0.0).astype(BF16)


def _s5_kernel(u_ref, ab_ref, ac_ref, kc_ref, k_ref, d_ref, o_ref,
               wb_ref, wc_ref, t_ref, h_ref, carry_ref):
    m, rcb = u_ref.shape
    cb = t_ref.shape[1]
    nr = rcb // cb
    w = h_ref.shape[1] // 2
    ns = SUBLANES

    @pl.when(pl.program_id(1) == 0)
    def _():
        carry_ref[...] = jnp.zeros_like(carry_ref)
        _s5_expand_weights(ab_ref, ac_ref, kc_ref, wb_ref, wc_ref, t_ref)

    h_ref[...] = _dot(u_ref[...], wb_ref[...])
    first_row = lax.broadcasted_iota(jnp.int32, (ns, w), 0) == 0

    def scan(r, carry):
        cr, ci = carry
        rows = pl.ds(pl.multiple_of(r * ns, ns), ns)
        xr = h_ref[rows, :w]
        xi = h_ref[rows, w:]
        for idx, shift in enumerate((1, 2, 4)):
            pr, pi = _cmul(k_ref[2 * idx], k_ref[2 * idx + 1],
                           pltpu.roll(xr, shift, 0), pltpu.roll(xi, shift, 0))
            xr, xi = xr + pr, xi + pi
        pr, pi = _cmul(k_ref[6], k_ref[7], cr, ci)
        xr, xi = xr + pr, xi + pi
        h_ref[rows, :w] = jnp.where(first_row, cr, pltpu.roll(xr, 1, 0))
        h_ref[rows, w:] = jnp.where(first_row, ci, pltpu.roll(xi, 1, 0))
        return xr[ns - 1:ns, :], xi[ns - 1:ns, :]

    cr, ci = lax.fori_loop(0, m // ns, scan, (carry_ref[0:1, :w], carry_ref[0:1, w:]))
    carry_ref[0:1, :w] = cr
    carry_ref[0:1, w:] = ci

    y_state = _dot(h_ref[...].astype(BF16), wc_ref[...])
    for qo in range(nr):
        lo = slice(qo * cb, (qo + 1) * cb)
        acc = y_state[:, lo] + d_ref[:, lo] * u_ref[:, lo].astype(F32)
        for qi in range(qo + 1):
            acc = acc + _dot(u_ref[:, qi * cb:(qi + 1) * cb], t_ref[(qo - qi) * cb:(qo - qi + 1) * cb, :])
        o_ref[:, lo] = jax.nn.gelu(acc).astype(o_ref.dtype)


def _s5_constants(lam_re, lam_im, log_dt, b_re, b_im, c_re, c_im):
    hi = lax.Precision.HIGHEST
    nl, g5, n5 = lam_re.shape
    ch = S5_GROUP_CH
    nr = S5_BLOCK
    gpb = min(S5_GROUPS_PER_BLOCK, g5)
    nblk = g5 // gpb
    dt = jnp.exp(log_dt)[..., None]
    mag = jnp.exp(lam_re * dt)
    ang = lam_im * dt
    abar_re, abar_im = mag * jnp.cos(ang), mag * jnp.sin(ang)
    den = lam_re * lam_re + lam_im * lam_im
    nre, nim = abar_re - 1.0, abar_im
    coef_re = (nre * lam_re + nim * lam_im) / den
    coef_im = (nim * lam_re - nre * lam_im) / den
    bbar_re = coef_re[..., None] * b_re - coef_im[..., None] * b_im
    bbar_im = coef_re[..., None] * b_im + coef_im[..., None] * b_re

    def powers(pr, pi, count):
        out = [(jnp.ones_like(pr), jnp.zeros_like(pi))]
        for _ in range(count):
            out.append(_cmul(out[-1][0], out[-1][1], pr, pi))
        return jnp.stack([o[0] for o in out], axis=1), jnp.stack([o[1] for o in out], axis=1)

    ap_re, ap_im = powers(abar_re, abar_im, nr)
    bp_re, bp_im = powers(ap_re[:, nr], ap_im[:, nr], SUBLANES)

    ab_re, ab_im = _cmul(ap_re[:, :nr, :, :, None], ap_im[:, :nr, :, :, None],
                         bbar_re[:, None], bbar_im[:, None])
    taps = (jnp.einsum('lgon,ldgni->ldgoi', c_re, ab_re, precision=hi)
            - jnp.einsum('lgon,ldgni->ldgoi', c_im, ab_im, precision=hi))
    kc = taps.reshape(nl, nr, nblk, gpb, ch, ch).transpose(0, 2, 1, 3, 5, 4)
    kc = jnp.pad(kc.reshape(nl, nblk, nr * gpb * ch, ch), ((0, 0), (0, 0), (0, 0), (0, LANES - ch)))

    ab = jnp.stack([ab_re[:, ::-1], ab_im[:, ::-1]], axis=2)
    ab = ab.reshape(nl, nr, 2, nblk, gpb, n5, ch).transpose(0, 3, 1, 4, 6, 2, 5)
    ab = ab.reshape(nl, nblk, nr * gpb * ch, 2 * n5)

    ca_re, ca_im = _cmul(c_re[:, None], c_im[:, None],
                         ap_re[:, 1:, :, None, :], ap_im[:, 1:, :, None, :])
    ac = jnp.stack([ca_re, -ca_im], axis=2)
    ac = ac.reshape(nl, nr, 2, nblk, gpb, ch, n5).transpose(0, 3, 2, 6, 1, 4, 5)
    ac = ac.reshape(nl, nblk, 2 * n5, nr * gpb * ch)

    def lanes(v):
        return v.reshape(nl, v.shape[1], nblk, gpb * n5).transpose(0, 2, 1, 3)

    rows = jnp.arange(SUBLANES)[:, None]
    kinds = []
    for shift in (1, 2, 4):
        for part in (bp_re, bp_im):
            kinds.append(jnp.where(rows >= shift, lanes(part[:, shift:shift + 1]), 0.0))
    for part in (bp_re, bp_im):
        kinds.append(lanes(part[:, 1:]))
    kconst = jnp.stack(kinds, axis=2)
    return ab.astype(BF16), ac.astype(BF16), kc.astype(BF16), kconst


def _s5_branch(u, consts, layer, d, blocks_per_step=512):
    nblk, s, cb = u.shape
    nr = S5_BLOCK
    ab, ac, kc, kconst = consts
    w2 = 2 * kconst.shape[4]
    assert s % nr == 0 and ab.shape[1:3] == (nblk, nr * cb)
    m = _pick_tile(s // nr, blocks_per_step)
    u2 = u.reshape(nblk, s // nr, nr * cb)
    d2 = jnp.tile(d.reshape(nblk, 1, cb), (1, 1, nr))
    out = pl.pallas_call(
        _s5_kernel,
        grid=(nblk, s // nr // m),
        in_specs=[pl.BlockSpec((None, m, nr * cb), lambda gb, t: (gb, t, 0)),
                  pl.BlockSpec((None, None) + ab.shape[2:], lambda gb, t: (layer, gb, 0, 0)),
                  pl.BlockSpec((None, None) + ac.shape[2:], lambda gb, t: (layer, gb, 0, 0)),
                  pl.BlockSpec((None, None) + kc.shape[2:], lambda gb, t: (layer, gb, 0, 0)),
                  pl.BlockSpec((None, None, 8, SUBLANES, w2 // 2), lambda gb, t: (layer, gb, 0, 0, 0)),
                  pl.BlockSpec((None, 1, nr * cb), lambda gb, t: (gb, 0, 0))],
        out_specs=pl.BlockSpec((None, m, nr * cb), lambda gb, t: (gb, t, 0)),
        out_shape=jax.ShapeDtypeStruct((nblk, s // nr, nr * cb), BF16),
        scratch_shapes=[pltpu.VMEM((nr * cb, w2), BF16), pltpu.VMEM((w2, nr * cb), BF16),
                        pltpu.VMEM((nr * cb, cb), BF16),
                        pltpu.VMEM((m, w2), F32), pltpu.VMEM((SUBLANES, w2), F32)],
        compiler_params=_compiler_params(("parallel", "arbitrary")),
        name="s5_branch",
    )(u2, ab, ac, kc, kconst, d2)
    return out.reshape(nblk, s, cb)


def kernel(x, c, ada_w1, ada_w2, ada_b, ada_table, norm_mix_w, w_in, ssd_conv_w, ssd_conv_b,
           ssd_dt_bias, ssd_a_log, ssd_d, ssd_norm_w, w_ssd_out, s5_lam_re, s5_lam_im, s5_log_dt,
           s5_b_re, s5_b_im, s5_c_re, s5_c_im, s5_d, w_glu, w_out, norm_ffn_w, w_up, ffn_conv_w,
           ffn_conv_b, w_down, final_norm_w):
    bsz, seq, dm = x.shape
    assert bsz == 1
    depth = w_in.shape[0]
    heads = ssd_dt_bias.shape[1]
    inner = heads * SSD_HEAD_DIM
    xbc_w = inner + 2 * SSD_GROUPS * SSD_STATE
    w5 = s5_d.shape[1]
    o_dt = inner + xbc_w
    o_u5 = o_dt + heads
    s5_cb = min(S5_GROUPS_PER_BLOCK, w5 // S5_GROUP_CH) * S5_GROUP_CH

    cpad = jnp.pad(jax.nn.silu(c), ((0, 2 * SUBLANES - bsz), (0, 0))).astype(BF16)
    t1 = _matmul(cpad, ada_w1.astype(BF16), BF16)
    mod = (_matmul(t1, ada_w2.astype(BF16), F32)[0] + ada_b).reshape(N_MOD, dm)

    s5_consts = _s5_constants(s5_lam_re, s5_lam_im, s5_log_dt, s5_b_re, s5_b_im, s5_c_re, s5_c_im)
    w_in_b, w_ssd_out_b, w_glu_b, w_out_b, w_up_b, w_down_b = (
        w.astype(BF16) for w in (w_in, w_ssd_out, w_glu, w_out, w_up, w_down))

    xs = x[0]
    for l in range(depth):
        m = mod + ada_table[l]
        h = _norm_mod(xs, norm_mix_w[l] * (1.0 + m[1]), m[0], BF16)
        w_dt = jnp.pad(w_in_b[l, :, o_dt:o_u5], ((0, 0), (0, LANES - heads)))
        proj = _matmul(h, w_in_b, BF16, layer=l, n=o_dt)
        gate_logits = _matmul(h, w_in_b[l, :, o_u5 + w5:], BF16, gate=True)
        u5 = _matmul_slabs(h, w_in_b[l, :, o_u5:o_u5 + w5], s5_cb, BF16)
        dt_raw = _matmul(h, w_dt, F32)
        g_ssd = _ssd_branch(proj, dt_raw, ssd_conv_w[l], ssd_conv_b[l], ssd_dt_bias[l],
                            ssd_a_log[l], ssd_d[l], ssd_norm_w[l], inner)
        g_s5 = _s5_branch(u5, s5_consts, l, s5_d[l])
        y_s5 = _matmul_glu(g_s5, w_glu_b, l, BF16)
        mix = _matmul_mix(g_ssd, w_ssd_out_b, l, gate_logits, 0, y_s5, BF16)
        xs = _matmul_resid(mix, w_out_b, l, xs, m[2])
        h = _norm_mod(xs, norm_ffn_w[l] * (1.0 + m[4]), m[3], BF16)
        act = _matmul_convgate(h, w_up_b, l, ffn_conv_w[l], ffn_conv_b[l], BF16)
        xs = _matmul_resid(act, w_down_b, l, xs, m[5])
    out = _norm_mod(xs, final_norm_w, jnp.zeros_like(final_norm_w), F32)
    return out[None]
```
